```python
import math
import jax, jax.numpy as jnp
from jax import lax
import numpy as np

D_MODEL = 1024
BATCH = 8
SEQ = 8192
DEPTH = 4

D_MIX = D_MODEL
DN_HEADS = 4
DN_HEAD_DIM = 128
DN_WIDTH = DN_HEADS * DN_HEAD_DIM
DN_CONV = 4
CHUNK = 64
SC_WIDTH = D_MIX - DN_WIDTH
SC_GROUPS = 4
SC_GROUP_DIM = SC_WIDTH // SC_GROUPS
SC_CONV = 3
D_FF = ((8 * D_MODEL + 3 * 256 - 1) // (3 * 256)) * 256
W_IN_COLS = 4 * DN_WIDTH + 2 * DN_HEADS + 3 * SC_WIDTH
EPS = 1e-6

kernel_name = 'hybrid_gdn_shortconv_swiglu'


def rms_norm(x, gain):
    xf = x.astype(jnp.float32)
    y = xf * lax.rsqrt(jnp.mean(xf * xf, axis=-1, keepdims=True) + EPS)
    return (y * gain.astype(jnp.float32)).astype(x.dtype)


def l2_normalize(x):
    return x * lax.rsqrt(jnp.sum(x * x, axis=-1, keepdims=True) + EPS)


def causal_depthwise_conv(x, w):
    K = w.shape[0]
    T = x.shape[1]
    xp = jnp.pad(x, ((0, 0), (K - 1, 0), (0, 0)))
    y = xp[:, 0:T, :] * w[0]
    for j in range(1, K):
        y = y + xp[:, j:j + T, :] * w[j]
    return y


def chunk_gated_delta_rule(q, k, v, g, beta):
    Bsz, T, H, DK = q.shape
    DV = v.shape[-1]
    C = CHUNK
    N = T // C
    q = q * (DK ** -0.5)

    def to_chunks(t):
        return t.reshape(Bsz, N, C, H, t.shape[-1]).transpose(0, 3, 1, 2, 4)

    q, k, v = to_chunks(q), to_chunks(k), to_chunks(v)
    g = jnp.cumsum(g.reshape(Bsz, N, C, H).transpose(0, 3, 1, 2), axis=-1)
    beta = beta.reshape(Bsz, N, C, H).transpose(0, 3, 1, 2)

    causal = jnp.tril(jnp.ones((C, C), dtype=bool))
    strict = jnp.tril(jnp.ones((C, C), dtype=bool), -1)
    decay = jnp.exp(jnp.where(causal, g[..., :, None] - g[..., None, :], -jnp.inf))

    k_beta = k * beta[..., None]
    v_beta = v * beta[..., None]
    lower = jnp.where(strict, jnp.einsum('bhncd,bhnmd->bhncm', k_beta, k) * decay, 0.0)
    a_mat = lower + jnp.eye(C, dtype=jnp.float32)
    rhs = jnp.concatenate([v_beta, k_beta * jnp.exp(g)[..., None]], axis=-1)
    sol = lax.linalg.triangular_solve(a_mat, rhs, left_side=True, lower=True, unit_diagonal=True)
    u, w = sol[..., :DV], sol[..., DV:]

    qk = jnp.einsum('bhncd,bhnmd->bhncm', q, k) * decay
    q_dec = q * jnp.exp(g)[..., None]
    k_dec = k * jnp.exp(g[..., -1:] - g)[..., None]
    g_last = jnp.exp(g[..., -1])

    def step(S, xs):
        qk_i, q_dec_i, k_dec_i, u_i, w_i, gl_i = xs
        v_new = u_i - jnp.einsum('bhck,bhkv->bhcv', w_i, S)
        o_i = jnp.einsum('bhck,bhkv->bhcv', q_dec_i, S) + jnp.einsum('bhcm,bhmv->bhcv', qk_i, v_new)
        S = S * gl_i[..., None, None] + jnp.einsum('bhck,bhcv->bhkv', k_dec_i, v_new)
        return S, o_i

    xs = (jnp.moveaxis(qk, 2, 0), jnp.moveaxis(q_dec, 2, 0), jnp.moveaxis(k_dec, 2, 0),
          jnp.moveaxis(u, 2, 0), jnp.moveaxis(w, 2, 0), jnp.moveaxis(g_last, 2, 0))
    S0 = jnp.zeros((Bsz, H, DK, DV), dtype=jnp.float32)
    _, o = lax.scan(step, S0, xs)
    return o.transpose(1, 0, 3, 2, 4).reshape(Bsz, T, H, DV)


def hybrid_layer(x, norm1_g, w_in, dn_conv_w, dn_a_log, dn_dt_bias, dn_norm_g,
                 sc_conv_w, sc_norm_g, w_out, norm2_g, ffn_w_gate, ffn_w_up, ffn_w_down):
    Bsz, T, _ = x.shape
    h = rms_norm(x, norm1_g)
    proj = h @ w_in
    o1 = 3 * DN_WIDTH
    o2 = o1 + DN_WIDTH
    o3 = o2 + DN_HEADS
    o4 = o3 + DN_HEADS
    qkv, z, b_in, a_in, sc_in = proj[..., :o1], proj[..., o1:o2], proj[..., o2:o3], proj[..., o3:o4], proj[..., o4:]

    qkv = jax.nn.silu(causal_depthwise_conv(qkv, dn_conv_w)).astype(jnp.float32)
    q = l2_normalize(qkv[..., :DN_WIDTH].reshape(Bsz, T, DN_HEADS, DN_HEAD_DIM))
    k = l2_normalize(qkv[..., DN_WIDTH:2 * DN_WIDTH].reshape(Bsz, T, DN_HEADS, DN_HEAD_DIM))
    v = qkv[..., 2 * DN_WIDTH:].reshape(Bsz, T, DN_HEADS, DN_HEAD_DIM)
    beta = jax.nn.sigmoid(b_in.astype(jnp.float32))
    g = -jnp.exp(dn_a_log.astype(jnp.float32)) * jax.nn.softplus(
        a_in.astype(jnp.float32) + dn_dt_bias.astype(jnp.float32))
    o_dn = chunk_gated_delta_rule(q, k, v, g, beta)
    zf = z.astype(jnp.float32).reshape(Bsz, T, DN_HEADS, DN_HEAD_DIM)
    o_dn = (o_dn * lax.rsqrt(jnp.mean(o_dn * o_dn, axis=-1, keepdims=True) + EPS)
            * dn_norm_g.astype(jnp.float32) * jax.nn.silu(zf))
    o_dn = o_dn.reshape(Bsz, T, DN_WIDTH).astype(x.dtype)

    gate_b, gate_c, hv = sc_in[..., :SC_WIDTH], sc_in[..., SC_WIDTH:2 * SC_WIDTH], sc_in[..., 2 * SC_WIDTH:]
    y = gate_b * causal_depthwise_conv(gate_c * hv, sc_conv_w)
    yf = y.astype(jnp.float32).reshape(Bsz, T, SC_GROUPS, SC_GROUP_DIM)
    yf = yf * lax.rsqrt(jnp.mean(yf * yf, axis=-1, keepdims=True) + EPS)
    o_sc = (yf * sc_norm_g.astype(jnp.float32).reshape(SC_GROUPS, SC_GROUP_DIM)).reshape(Bsz, T, SC_WIDTH).astype(x.dtype)

    x = x + jnp.concatenate([o_dn, o_sc], axis=-1) @ w_out

    h2 = rms_norm(x, norm2_g)
    x = x + (jax.nn.silu(h2 @ ffn_w_gate) * (h2 @ ffn_w_up)) @ ffn_w_down
    return x


def setup_inputs(seed: int = 0) -> dict:
    key = jax.random.key(seed)
    ks = jax.random.split(key, 16)
    f32 = jnp.float32

    def nrm(k, shape, scale):
        return jax.random.normal(k, shape, f32) * scale

    def gain(k, shape):
        return 1.0 + 0.02 * jax.random.normal(k, shape, f32)

    x = nrm(ks[0], (BATCH, SEQ, D_MODEL), 1.0)
    norm1_g = gain(ks[1], (DEPTH, D_MODEL))
    w_in = nrm(ks[2], (DEPTH, D_MODEL, W_IN_COLS), D_MODEL ** -0.5)
    dn_conv_w = nrm(ks[3], (DEPTH, DN_CONV, 3 * DN_WIDTH), DN_CONV ** -0.5)
    dn_a_log = jnp.log(jax.random.uniform(ks[4], (DEPTH, DN_HEADS), f32, 1.0, 16.0))
    dt = jnp.exp(jax.random.uniform(ks[5], (DEPTH, DN_HEADS), f32, math.log(1e-3), math.log(1e-1)))
    dn_dt_bias = dt + jnp.log(-jnp.expm1(-dt))
    dn_norm_g = gain(ks[6], (DEPTH, DN_HEAD_DIM))
    sc_conv_w = nrm(ks[7], (DEPTH, SC_CONV, SC_WIDTH), SC_CONV ** -0.5)
    sc_norm_g = gain(ks[8], (DEPTH, SC_WIDTH))
    w_out = nrm(ks[9], (DEPTH, D_MIX, D_MODEL), D_MIX ** -0.5)
    norm2_g = gain(ks[10], (DEPTH, D_MODEL))
    ffn_w_gate = nrm(ks[11], (DEPTH, D_MODEL, D_FF), D_MODEL ** -0.5)
    ffn_w_up = nrm(ks[12], (DEPTH, D_MODEL, D_FF), D_MODEL ** -0.5)
    ffn_w_down = nrm(ks[13], (DEPTH, D_FF, D_MODEL), D_FF ** -0.5)
    final_norm_g = gain(ks[14], (D_MODEL,))
    return {'x': x, 'norm1_g': norm1_g, 'w_in': w_in, 'dn_conv_w': dn_conv_w,
            'dn_a_log': dn_a_log, 'dn_dt_bias': dn_dt_bias, 'dn_norm_g': dn_norm_g,
            'sc_conv_w': sc_conv_w, 'sc_norm_g': sc_norm_g, 'w_out': w_out,
            'norm2_g': norm2_g, 'ffn_w_gate': ffn_w_gate, 'ffn_w_up': ffn_w_up,
            'ffn_w_down': ffn_w_down, 'final_norm_g': final_norm_g}


def reference(x, norm1_g, w_in, dn_conv_w, dn_a_log, dn_dt_bias, dn_norm_g, sc_conv_w,
              sc_norm_g, w_out, norm2_g, ffn_w_gate, ffn_w_up, ffn_w_down, final_norm_g):
    for l in range(DEPTH):
        x = hybrid_layer(x, norm1_g[l], w_in[l], dn_conv_w[l], dn_a_log[l], dn_dt_bias[l],
                         dn_norm_g[l], sc_conv_w[l], sc_norm_g[l], w_out[l], norm2_g[l],
                         ffn_w_gate[l], ffn_w_up[l], ffn_w_down[l])
    return rms_norm(x, final_norm_g)
```

```python
import functools

import jax
import jax.numpy as jnp
from jax import lax
from jax.experimental import pallas as pl
from jax.experimental.pallas import tpu as pltpu

F32 = jnp.float32
BF16 = jnp.bfloat16

EPS = 1e-6
LANES = 128
SUBLANES = 8
DN_HEADS = 4
HEAD_DIM = 128
DN_WIDTH = DN_HEADS * HEAD_DIM
DN_CONV = 4
SC_GROUPS = 4
SC_CONV = 3
CHUNK = 64
QKV_COLS = 3 * DN_WIDTH
Z_OFF = QKV_COLS
SC_OFF = Z_OFF + DN_WIDTH
BETA_LANE = 0
DECAY_LANE = DN_HEADS

MIXER_TIME_TILE = 512
FFN_ROW_TILE = 512
MIXER_VMEM_BYTES = 52 * 1024 * 1024
FFN_VMEM_BYTES = 56 * 1024 * 1024


def _sigmoid(v):
    return 1.0 / (1.0 + jnp.exp(-v))


def _silu(v):
    return v * _sigmoid(v)


def _softplus(v):
    return jnp.maximum(v, 0.0) + jnp.log1p(jnp.exp(-jnp.abs(v)))


def _mm(a, b):
    return jnp.dot(a.astype(BF16), b.astype(BF16), preferred_element_type=F32)


def _mm_nt(a, b):
    return lax.dot_general(a.astype(BF16), b.astype(BF16), (((1,), (1,)), ((), ())),
                           preferred_element_type=F32)


def _mm_tn(a, b):
    return lax.dot_general(a.astype(BF16), b.astype(BF16), (((0,), (0,)), ((), ())),
                           preferred_element_type=F32)


def _split3(v):
    hi = v.astype(BF16)
    r1 = v - hi.astype(F32)
    mid = r1.astype(BF16)
    lo = (r1 - mid.astype(F32)).astype(BF16)
    return hi, mid, lo


def _mm_split(a, b):
    a_hi = a.astype(BF16)
    a_lo = (a - a_hi.astype(F32)).astype(BF16)
    b_hi = b.astype(BF16)
    b_lo = (b - b_hi.astype(F32)).astype(BF16)
    d = lambda p, q: jnp.dot(p, q, preferred_element_type=F32)
    return d(a_hi, b_hi) + (d(a_hi, b_lo) + d(a_lo, b_hi))


def _unit_lower_inverse(lower):
    c = lower.shape[0]
    eye = (lax.broadcasted_iota(jnp.int32, (c, c), 0) ==
           lax.broadcasted_iota(jnp.int32, (c, c), 1)).astype(F32)
    m = -lower
    p = eye + m
    power = 1
    while 2 * power < c:
        m = _mm_split(m, m)
        p = p + _mm_split(p, m)
        power *= 2
    return p


def _rms_scale(v, width):
    return lax.rsqrt(jnp.sum(v * v, axis=-1, keepdims=True) * (1.0 / width) + EPS)


def _mixer_kernel(x_ref, n1g_ref, win_ref, cw_ref, gp_ref, dng_ref, scw_ref, scg_ref, wout_ref,
                  out_ref, qkv_buf, z_buf, scin_buf, small_buf, cv_buf, o_buf, s_ref, *, tt):
    t = pl.program_id(1)
    sc_width = scg_ref.shape[-1]
    pad = SUBLANES

    @pl.when(t == 0)
    def _():
        s_ref[...] = jnp.zeros_like(s_ref)
        qkv_buf[0:pad, :] = jnp.zeros((pad, QKV_COLS), F32)
        cv_buf[0:pad, :] = jnp.zeros((pad, sc_width), F32)

    x = x_ref[...]
    d_model = x.shape[-1]
    hb = (x * _rms_scale(x, d_model) * n1g_ref[...]).astype(BF16)
    qkv_buf[pad:pad + tt, :] = jnp.dot(hb, win_ref[:, 0:QKV_COLS], preferred_element_type=F32)
    z_buf[...] = jnp.dot(hb, win_ref[:, Z_OFF:Z_OFF + DN_WIDTH], preferred_element_type=F32)
    scin_buf[...] = jnp.dot(hb, win_ref[:, SC_OFF:SC_OFF + 3 * sc_width], preferred_element_type=F32)
    small_off = SC_OFF + 3 * sc_width
    small_buf[...] = jnp.dot(hb, win_ref[:, small_off:small_off + LANES], preferred_element_type=F32)

    row = lax.broadcasted_iota(jnp.int32, (CHUNK, CHUNK), 0)
    col = lax.broadcasted_iota(jnp.int32, (CHUNK, CHUNK), 1)
    causal = row >= col
    strict = row > col
    tri = causal.astype(BF16)
    q_scale = HEAD_DIM ** -0.5
    neg_rate = -jnp.exp(gp_ref[0:1, :])
    dt_bias = gp_ref[1:2, :]

    def chunk_body(c, carry):
        r0 = pl.multiple_of(c * CHUNK, CHUNK)
        rows = pl.ds(r0, CHUNK)

        sm = small_buf[rows, :]
        beta_blk = _sigmoid(sm)
        g_raw = neg_rate * _softplus(sm + dt_bias)
        g_hi, g_mid, g_lo = _split3(g_raw)
        cum = lambda part: jnp.dot(tri, part, preferred_element_type=F32)
        g_blk = cum(g_hi) + (cum(g_mid) + cum(g_lo))
        g_t = g_blk.T
        eg_blk = jnp.exp(g_blk)
        g_last_blk = jnp.broadcast_to(g_blk[CHUNK - 1:CHUNK, :], (CHUNK, LANES))
        kdec_blk = jnp.exp(g_last_blk - g_blk)

        for h in range(DN_HEADS):
            def conv_silu(col0):
                win = qkv_buf[pl.ds(r0, CHUNK + pad), col0:col0 + HEAD_DIM]
                first = pad - (DN_CONV - 1)
                acc = cw_ref[0:1, col0:col0 + HEAD_DIM] * win[first:first + CHUNK]
                for j in range(1, DN_CONV):
                    acc = acc + cw_ref[j:j + 1, col0:col0 + HEAD_DIM] * win[first + j:first + j + CHUNK]
                return _silu(acc)

            qh = conv_silu(h * HEAD_DIM)
            kh = conv_silu(DN_WIDTH + h * HEAD_DIM)
            vh = conv_silu(2 * DN_WIDTH + h * HEAD_DIM)
            qn = qh * (lax.rsqrt(jnp.sum(qh * qh, axis=-1, keepdims=True) + EPS) * q_scale)
            kn = kh * lax.rsqrt(jnp.sum(kh * kh, axis=-1, keepdims=True) + EPS)

            bl = BETA_LANE + h
            dl = DECAY_LANE + h
            beta_c = beta_blk[:, bl:bl + 1]
            g_c = g_blk[:, dl:dl + 1]
            eg_c = eg_blk[:, dl:dl + 1]
            kdec_c = kdec_blk[:, dl:dl + 1]
            g_r = g_t[dl:dl + 1, :]
            decay = jnp.exp(jnp.where(causal, g_c - g_r, -jnp.inf))

            kb = kn * beta_c
            vb = vh * beta_c
            lower = jnp.where(strict, _mm_nt(kb, kn) * decay, 0.0)
            t_inv = _unit_lower_inverse(lower)
            sol = _mm(t_inv, jnp.concatenate([vb, kb * eg_c], axis=-1))
            u = sol[:, :HEAD_DIM]
            w = sol[:, HEAD_DIM:]
            qk = _mm_nt(qn, kn) * decay
            q_dec = qn * eg_c
            k_dec = kn * kdec_c

            s = s_ref[h]
            ws = _mm(jnp.concatenate([w, q_dec], axis=0), s)
            v_new = u - ws[:CHUNK]
            o = ws[CHUNK:] + _mm(qk, v_new)
            s_ref[h] = s * eg_blk[CHUNK - 1:CHUNK, dl:dl + 1] + _mm_tn(k_dec, v_new)

            zh = z_buf[rows, h * HEAD_DIM:(h + 1) * HEAD_DIM]
            o = o * _rms_scale(o, HEAD_DIM) * dng_ref[...] * _silu(zh)
            o_buf[rows, h * HEAD_DIM:(h + 1) * HEAD_DIM] = o.astype(BF16)

        cv = scin_buf[rows, sc_width:2 * sc_width] * scin_buf[rows, 2 * sc_width:3 * sc_width]
        cv_buf[pl.ds(r0 + pad, CHUNK), :] = cv
        cwin = cv_buf[pl.ds(r0, CHUNK + pad), :]
        first = pad - (SC_CONV - 1)
        acc = scw_ref[SC_CONV - 1:SC_CONV, :] * cv
        for j in range(SC_CONV - 1):
            acc = acc + scw_ref[j:j + 1, :] * cwin[first + j:first + j + CHUNK]
        y = scin_buf[rows, 0:sc_width] * acc
        gw = sc_width // SC_GROUPS
        for g in range(SC_GROUPS):
            yg = y[:, g * gw:(g + 1) * gw]
            yg = yg * _rms_scale(yg, gw) * scg_ref[:, g * gw:(g + 1) * gw]
            o_buf[rows, DN_WIDTH + g * gw:DN_WIDTH + (g + 1) * gw] = yg.astype(BF16)
        return carry

    lax.fori_loop(0, tt // CHUNK, chunk_body, 0)

    out_ref[...] = x_ref[...] + jnp.dot(o_buf[...], wout_ref[...], preferred_element_type=F32)
    qkv_buf[0:pad, :] = qkv_buf[tt:tt + pad, :]
    cv_buf[0:pad, :] = cv_buf[tt:tt + pad, :]


def _ffn_kernel(x_ref, n2g_ref, wg_ref, wu_ref, wd_ref, fg_ref, out_ref, *, final_norm):
    x = x_ref[...]
    d_model = x.shape[-1]
    hb = (x * _rms_scale(x, d_model) * n2g_ref[...]).astype(BF16)
    gate = jnp.dot(hb, wg_ref[...], preferred_element_type=F32)
    up = jnp.dot(hb, wu_ref[...], preferred_element_type=F32)
    act = (_silu(gate) * up).astype(BF16)
    y = x + jnp.dot(act, wd_ref[...], preferred_element_type=F32)
    if final_norm:
        y = y * _rms_scale(y, d_model) * fg_ref[...]
    out_ref[...] = y


def _resident(shape, index_map):
    return pl.BlockSpec(shape, index_map, pipeline_mode=pl.Buffered(1))


def _mixer_call(x, n1g, win, cw, gp, dng, scw, scg, wout, layer):
    bsz, seq, d_model = x.shape
    tt = min(MIXER_TIME_TILE, seq)
    assert seq % tt == 0 and tt % CHUNK == 0
    sc_width = scg.shape[-1]
    in_cols = win.shape[-1]
    pad = SUBLANES
    lay = lambda *rest: (lambda b, t: (layer, *rest))
    return pl.pallas_call(
        functools.partial(_mixer_kernel, tt=tt),
        grid=(bsz, seq // tt),
        in_specs=[
            pl.BlockSpec((None, tt, d_model), lambda b, t: (b, t, 0)),
            _resident((None, 1, d_model), lay(0, 0)),
            _resident((None, d_model, in_cols), lay(0, 0)),
            _resident((None, DN_CONV, QKV_COLS), lay(0, 0)),
            _resident((None, SUBLANES, LANES), lay(0, 0)),
            _resident((None, 1, HEAD_DIM), lay(0, 0)),
            _resident((None, SC_CONV, sc_width), lay(0, 0)),
            _resident((None, 1, sc_width), lay(0, 0)),
            _resident((None, d_model, d_model), lay(0, 0)),
        ],
        out_specs=pl.BlockSpec((None, tt, d_model), lambda b, t: (b, t, 0)),
        out_shape=jax.ShapeDtypeStruct(x.shape, F32),
        scratch_shapes=[
            pltpu.VMEM((tt + pad, QKV_COLS), F32),
            pltpu.VMEM((tt, DN_WIDTH), F32),
            pltpu.VMEM((tt, 3 * sc_width), F32),
            pltpu.VMEM((tt, LANES), F32),
            pltpu.VMEM((tt + pad, sc_width), F32),
            pltpu.VMEM((tt, d_model), BF16),
            pltpu.VMEM((DN_HEADS, HEAD_DIM, HEAD_DIM), F32),
        ],
        compiler_params=pltpu.CompilerParams(
            dimension_semantics=("arbitrary", "arbitrary"),
            vmem_limit_bytes=MIXER_VMEM_BYTES),
        name=f"mixer_l{layer}",
    )(x, n1g, win, cw, gp, dng, scw, scg, wout)


def _ffn_call(x2d, n2g, wg, wu, wd, fg, layer, final_norm):
    rows, d_model = x2d.shape
    tm = min(FFN_ROW_TILE, rows)
    assert rows % tm == 0
    d_ff = wg.shape[-1]
    lay = lambda *rest: (lambda i: (layer, *rest))
    return pl.pallas_call(
        functools.partial(_ffn_kernel, final_norm=final_norm),
        grid=(rows // tm,),
        in_specs=[
            pl.BlockSpec((tm, d_model), lambda i: (i, 0)),
            _resident((None, 1, d_model), lay(0, 0)),
            _resident((None, d_model, d_ff), lay(0, 0)),
            _resident((None, d_model, d_ff), lay(0, 0)),
            _resident((None, d_ff, d_model), lay(0, 0)),
            _resident((1, d_model), lambda i: (0, 0)),
        ],
        out_specs=pl.BlockSpec((tm, d_model), lambda i: (i, 0)),
        out_shape=jax.ShapeDtypeStruct(x2d.shape, F32),
        compiler_params=pltpu.CompilerParams(
            dimension_semantics=("arbitrary",),
            vmem_limit_bytes=FFN_VMEM_BYTES),
        name=f"ffn_l{layer}",
    )(x2d, n2g, wg, wu, wd, fg)


def kernel(x, norm1_g, w_in, dn_conv_w, dn_a_log, dn_dt_bias, dn_norm_g, sc_conv_w, sc_norm_g,
           w_out, norm2_g, ffn_w_gate, ffn_w_up, ffn_w_down, final_norm_g):
    depth, d_model, _ = w_in.shape
    bsz, seq, _ = x.shape
    sc_width = sc_norm_g.shape[-1]

    o1 = QKV_COLS + DN_WIDTH
    o2 = o1 + 2 * DN_HEADS
    small = jnp.pad(w_in[..., o1:o2], ((0, 0), (0, 0), (0, LANES - 2 * DN_HEADS)))
    win = jnp.concatenate([w_in[..., :o1], w_in[..., o2:], small], axis=-1).astype(BF16)
    wout = w_out.astype(BF16)
    wg = ffn_w_gate.astype(BF16)
    wu = ffn_w_up.astype(BF16)
    wd = ffn_w_down.astype(BF16)

    gp = jnp.zeros((depth, SUBLANES, LANES), F32)
    gp = gp.at[:, 0, DECAY_LANE:DECAY_LANE + DN_HEADS].set(dn_a_log.astype(F32))
    gp = gp.at[:, 1, DECAY_LANE:DECAY_LANE + DN_HEADS].set(dn_dt_bias.astype(F32))

    n1g = norm1_g.reshape(depth, 1, d_model)
    n2g = norm2_g.reshape(depth, 1, d_model)
    dng = dn_norm_g.reshape(depth, 1, HEAD_DIM)
    scg = sc_norm_g.reshape(depth, 1, sc_width)
    fg = final_norm_g.reshape(1, d_model)

    for layer in range(depth):
        x = _mixer_call(x, n1g, win, dn_conv_w, gp, dng, sc_conv_w, scg, wout, layer)
        x = _ffn_call(x.reshape(bsz * seq, d_model), n2g, wg, wu, wd, fg, layer,
                      final_norm=(layer == depth - 1)).reshape(bsz, seq, d_model)
    return x
```

```python
import functools

import jax
import jax.numpy as jnp
from jax import lax
from jax.experimental import pallas as pl
from jax.experimental.pallas import tpu as pltpu

F32 = jnp.float32
BF16 = jnp.bfloat16

EPS = 1e-6
LANES = 128
SUBLANES = 8
DN_HEADS = 4
HEAD_DIM = 128
DN_WIDTH = DN_HEADS * HEAD_DIM
DN_CONV = 4
SC_GROUPS = 4
SC_CONV = 3
CHUNK = 64
QKV_COLS = 3 * DN_WIDTH
Z_OFF = QKV_COLS
SC_OFF = Z_OFF + DN_WIDTH
BETA_LANE = 0
DECAY_LANE = DN_HEADS

MIXER_TIME_TILE = 512
FFN_ROW_TILE = 512
MIXER_VMEM_BYTES = 52 * 1024 * 1024
FFN_VMEM_BYTES = 56 * 1024 * 1024


def _sigmoid(v):
    return 1.0 / (1.0 + jnp.exp(-v))


def _silu(v):
    half = 0.5 * v
    return half + half * jnp.tanh(half)


def _softplus(v):
    return jnp.maximum(v, 0.0) + jnp.log1p(jnp.exp(-jnp.abs(v)))


def _split3(v):
    hi = v.astype(BF16)
    r1 = v - hi.astype(F32)
    mid = r1.astype(BF16)
    lo = (r1 - mid.astype(F32)).astype(BF16)
    return hi, mid, lo


def _bmm(a, b):
    return lax.dot_general(a.astype(BF16), b.astype(BF16), (((2,), (1,)), ((0,), (0,))),
                           preferred_element_type=F32)


def _bmm_nt(a, b):
    return lax.dot_general(a.astype(BF16), b.astype(BF16), (((2,), (2,)), ((0,), (0,))),
                           preferred_element_type=F32)


def _unit_lower_inverse(lower):
    c = lower.shape[-1]
    eye = (lax.broadcasted_iota(jnp.int32, (c, c), 0) ==
           lax.broadcasted_iota(jnp.int32, (c, c), 1)).astype(F32)
    m = -lower
    p = eye + m
    power = 1
    while 2 * power < c:
        m = _bmm(m, m)
        p = p + _bmm(p, m)
        power *= 2
    return p


def _rms_scale(v, width):
    return lax.rsqrt(jnp.sum(v * v, axis=-1, keepdims=True) * (1.0 / width) + EPS)


def _mixer_kernel(x_ref, n1g_ref, win_ref, cw_ref, gp_ref, dng_ref, scw_ref, scg_ref, wout_ref,
                  out_ref,
                  qkv_buf, z_buf, scin_buf, small_buf, cv_buf, o_buf, s_ref,
                  kn_buf, kbq_buf, rhs_buf, qd_buf, kdt_buf, decay_buf, glast_buf,
                  u_buf, w_buf, qkd_buf, pm_buf, qm_buf, state_buf, *, tt):
    t = pl.program_id(1)
    sc_width = scg_ref.shape[-1]
    pad = SUBLANES
    nc = tt // CHUNK

    @pl.when(t == 0)
    def _():
        s_ref[...] = jnp.zeros_like(s_ref)
        qkv_buf[0:pad, :] = jnp.zeros((pad, QKV_COLS), F32)
        cv_buf[0:pad, :] = jnp.zeros((pad, sc_width), F32)

    x = x_ref[...]
    d_model = x.shape[-1]
    hb = (x * _rms_scale(x, d_model) * n1g_ref[...]).astype(BF16)
    qkv_buf[pad:pad + tt, :] = jnp.dot(hb, win_ref[:, 0:QKV_COLS], preferred_element_type=F32)
    z_buf[...] = jnp.dot(hb, win_ref[:, Z_OFF:Z_OFF + DN_WIDTH], preferred_element_type=F32)
    scin_buf[...] = jnp.dot(hb, win_ref[:, SC_OFF:SC_OFF + 3 * sc_width], preferred_element_type=F32)
    small_off = SC_OFF + 3 * sc_width
    small_buf[...] = jnp.dot(hb, win_ref[:, small_off:small_off + LANES], preferred_element_type=F32)

    row = lax.broadcasted_iota(jnp.int32, (CHUNK, CHUNK), 0)
    col = lax.broadcasted_iota(jnp.int32, (CHUNK, CHUNK), 1)
    causal = row >= col
    strict = row > col
    tri = causal.astype(BF16)
    q_scale = HEAD_DIM ** -0.5
    neg_rate = -jnp.exp(gp_ref[0:1, :])
    dt_bias = gp_ref[1:2, :]

    def chunk_body(c, carry):
        r0 = pl.multiple_of(c * CHUNK, CHUNK)
        rows = pl.ds(r0, CHUNK)

        sm = small_buf[rows, :]
        beta_blk = _sigmoid(sm)
        g_raw = neg_rate * _softplus(sm + dt_bias)
        g_hi, g_mid, g_lo = _split3(g_raw)
        cum = lambda part: jnp.dot(tri, part, preferred_element_type=F32)
        g_blk = cum(g_hi) + (cum(g_mid) + cum(g_lo))
        g_t = g_blk.T
        eg_blk = jnp.exp(g_blk)
        g_last_blk = jnp.broadcast_to(g_blk[CHUNK - 1:CHUNK, :], (CHUNK, LANES))
        kdec_blk = jnp.exp(g_last_blk - g_blk)
        glast_buf[c] = eg_blk[CHUNK - SUBLANES:CHUNK, :]

        for h in range(DN_HEADS):
            def conv_silu(col0):
                win = qkv_buf[pl.ds(r0, CHUNK + pad), col0:col0 + HEAD_DIM]
                first = pad - (DN_CONV - 1)
                acc = cw_ref[0:1, col0:col0 + HEAD_DIM] * win[first:first + CHUNK]
                for j in range(1, DN_CONV):
                    acc = acc + cw_ref[j:j + 1, col0:col0 + HEAD_DIM] * win[first + j:first + j + CHUNK]
                return _silu(acc)

            qh = conv_silu(h * HEAD_DIM)
            kh = conv_silu(DN_WIDTH + h * HEAD_DIM)
            vh = conv_silu(2 * DN_WIDTH + h * HEAD_DIM)
            qn = qh * (lax.rsqrt(jnp.sum(qh * qh, axis=-1, keepdims=True) + EPS) * q_scale)
            kn = kh * lax.rsqrt(jnp.sum(kh * kh, axis=-1, keepdims=True) + EPS)

            bl = BETA_LANE + h
            dl = DECAY_LANE + h
            beta_c = beta_blk[:, bl:bl + 1]
            g_c = g_blk[:, dl:dl + 1]
            eg_c = eg_blk[:, dl:dl + 1]
            kdec_c = kdec_blk[:, dl:dl + 1]
            g_r = g_t[dl:dl + 1, :]
            decay_buf[h, c] = jnp.exp(jnp.where(causal, g_c - g_r, -jnp.inf))

            kb = kn * beta_c
            kn_buf[h, c] = kn.astype(BF16)
            kbq_buf[h, c, 0:CHUNK, :] = kb.astype(BF16)
            kbq_buf[h, c, CHUNK:2 * CHUNK, :] = qn.astype(BF16)
            rhs_buf[h, c, :, 0:HEAD_DIM] = (vh * beta_c).astype(BF16)
            rhs_buf[h, c, :, HEAD_DIM:2 * HEAD_DIM] = (kb * eg_c).astype(BF16)
            qd_buf[h, c] = (qn * eg_c).astype(BF16)
            kdt_buf[h, c] = (kn * kdec_c).T.astype(BF16)

            zh = z_buf[rows, h * HEAD_DIM:(h + 1) * HEAD_DIM]
            z_buf[rows, h * HEAD_DIM:(h + 1) * HEAD_DIM] = dng_ref[...] * _silu(zh)

        cv = scin_buf[rows, sc_width:2 * sc_width] * scin_buf[rows, 2 * sc_width:3 * sc_width]
        cv_buf[pl.ds(r0 + pad, CHUNK), :] = cv
        cwin = cv_buf[pl.ds(r0, CHUNK + pad), :]
        first = pad - (SC_CONV - 1)
        acc = scw_ref[SC_CONV - 1:SC_CONV, :] * cv
        for j in range(SC_CONV - 1):
            acc = acc + scw_ref[j:j + 1, :] * cwin[first + j:first + j + CHUNK]
        y = scin_buf[rows, 0:sc_width] * acc
        gw = sc_width // SC_GROUPS
        for g in range(SC_GROUPS):
            yg = y[:, g * gw:(g + 1) * gw]
            yg = yg * _rms_scale(yg, gw) * scg_ref[:, g * gw:(g + 1) * gw]
            o_buf[rows, DN_WIDTH + g * gw:DN_WIDTH + (g + 1) * gw] = yg.astype(BF16)
        return carry

    lax.fori_loop(0, nc, chunk_body, 0)

    for h in range(DN_HEADS):
        kkqk = _bmm_nt(kbq_buf[h], kn_buf[h])
        decay = decay_buf[h]
        lower = jnp.where(strict, kkqk[:, :CHUNK] * decay, 0.0)
        qkd_buf[h] = (kkqk[:, CHUNK:] * decay).astype(BF16)
        t_inv = _unit_lower_inverse(lower)
        sol = _bmm(t_inv, rhs_buf[h])
        u_buf[h] = sol[:, :, :HEAD_DIM]
        w_buf[h] = sol[:, :, HEAD_DIM:].astype(BF16)
        kds = _bmm(kdt_buf[h], sol)
        qm_buf[h] = kds[:, :, :HEAD_DIM]
        pm_buf[h] = (-kds[:, :, HEAD_DIM:]).astype(BF16)

    for c in range(nc):
        for h in range(DN_HEADS):
            dl = DECAY_LANE + h
            s = s_ref[h]
            sb = s.astype(BF16)
            state_buf[h, c] = sb
            gl = glast_buf[c, SUBLANES - 1:SUBLANES, dl:dl + 1]
            s_ref[h] = s * gl + (jnp.dot(pm_buf[h, c], sb, preferred_element_type=F32) + qm_buf[h, c])

    for h in range(DN_HEADS):
        wq = jnp.concatenate([w_buf[h], qd_buf[h]], axis=1)
        ws = _bmm(wq, state_buf[h])
        v_new = u_buf[h] - ws[:, :CHUNK]
        o = ws[:, CHUNK:] + _bmm(qkd_buf[h], v_new)
        o = o.reshape(tt, HEAD_DIM)
        o = o * _rms_scale(o, HEAD_DIM) * z_buf[:, h * HEAD_DIM:(h + 1) * HEAD_DIM]
        o_buf[:, h * HEAD_DIM:(h + 1) * HEAD_DIM] = o.astype(BF16)

    out_ref[...] = x_ref[...] + jnp.dot(o_buf[...], wout_ref[...], preferred_element_type=F32)
    qkv_buf[0:pad, :] = qkv_buf[tt:tt + pad, :]
    cv_buf[0:pad, :] = cv_buf[tt:tt + pad, :]


def _ffn_kernel(x_ref, n2g_ref, wg_ref, wu_ref, wd_ref, fg_ref, out_ref, *, final_norm):
    x = x_ref[...]
    d_model = x.shape[-1]
    hb = (x * _rms_scale(x, d_model) * n2g_ref[...]).astype(BF16)
    gate = jnp.dot(hb, wg_ref[...], preferred_element_type=F32)
    up = jnp.dot(hb, wu_ref[...], preferred_element_type=F32)
    act = (_silu(gate) * up).astype(BF16)
    y = x + jnp.dot(act, wd_ref[...], preferred_element_type=F32)
    if final_norm:
        y = y * _rms_scale(y, d_model) * fg_ref[...]
    out_ref[...] = y


def _resident(shape, index_map):
    return pl.BlockSpec(shape, index_map, pipeline_mode=pl.Buffered(1))


def _mixer_call(x, n1g, win, cw, gp, dng, scw, scg, wout, layer):
    bsz, seq, d_model = x.shape
    tt = min(MIXER_TIME_TILE, seq)
    assert seq % tt == 0 and tt % CHUNK == 0
    sc_width = scg.shape[-1]
    in_cols = win.shape[-1]
    pad = SUBLANES
    nc = tt // CHUNK
    lay = lambda *rest: (lambda b, t: (layer, *rest))
    return pl.pallas_call(
        functools.partial(_mixer_kernel, tt=tt),
        grid=(bsz, seq // tt),
        in_specs=[
            pl.BlockSpec((None, tt, d_model), lambda b, t: (b, t, 0)),
            _resident((None, 1, d_model), lay(0, 0)),
            _resident((None, d_model, in_cols), lay(0, 0)),
            _resident((None, DN_CONV, QKV_COLS), lay(0, 0)),
            _resident((None, SUBLANES, LANES), lay(0, 0)),
            _resident((None, 1, HEAD_DIM), lay(0, 0)),
            _resident((None, SC_CONV, sc_width), lay(0, 0)),
            _resident((None, 1, sc_width), lay(0, 0)),
            _resident((None, d_model, d_model), lay(0, 0)),
        ],
        out_specs=pl.BlockSpec((None, tt, d_model), lambda b, t: (b, t, 0)),
        out_shape=jax.ShapeDtypeStruct(x.shape, F32),
        scratch_shapes=[
            pltpu.VMEM((tt + pad, QKV_COLS), F32),
            pltpu.VMEM((tt, DN_WIDTH), F32),
            pltpu.VMEM((tt, 3 * sc_width), F32),
            pltpu.VMEM((tt, LANES), F32),
            pltpu.VMEM((tt + pad, sc_width), F32),
            pltpu.VMEM((tt, d_model), BF16),
            pltpu.VMEM((DN_HEADS, HEAD_DIM, HEAD_DIM), F32),
            pltpu.VMEM((DN_HEADS, nc, CHUNK, HEAD_DIM), BF16),
            pltpu.VMEM((DN_HEADS, nc, 2 * CHUNK, HEAD_DIM), BF16),
            pltpu.VMEM((DN_HEADS, nc, CHUNK, 2 * HEAD_DIM), BF16),
            pltpu.VMEM((DN_HEADS, nc, CHUNK, HEAD_DIM), BF16),
            pltpu.VMEM((DN_HEADS, nc, HEAD_DIM, CHUNK), BF16),
            pltpu.VMEM((DN_HEADS, nc, CHUNK, CHUNK), F32),
            pltpu.VMEM((nc, SUBLANES, LANES), F32),
            pltpu.VMEM((DN_HEADS, nc, CHUNK, HEAD_DIM), F32),
            pltpu.VMEM((DN_HEADS, nc, CHUNK, HEAD_DIM), BF16),
            pltpu.VMEM((DN_HEADS, nc, CHUNK, CHUNK), BF16),
            pltpu.VMEM((DN_HEADS, nc, HEAD_DIM, HEAD_DIM), BF16),
            pltpu.VMEM((DN_HEADS, nc, HEAD_DIM, HEAD_DIM), F32),
            pltpu.VMEM((DN_HEADS, nc, HEAD_DIM, HEAD_DIM), BF16),
        ],
        compiler_params=pltpu.CompilerParams(
            dimension_semantics=("arbitrary", "arbitrary"),
            vmem_limit_bytes=MIXER_VMEM_BYTES),
        name=f"mixer_l{layer}",
    )(x, n1g, win, cw, gp, dng, scw, scg, wout)


def _ffn_call(x2d, n2g, wg, wu, wd, fg, layer, final_norm):
    rows, d_model = x2d.shape
    tm = min(FFN_ROW_TILE, rows)
    assert rows % tm == 0
    d_ff = wg.shape[-1]
    lay = lambda *rest: (lambda i: (layer, *rest))
    return pl.pallas_call(
        functools.partial(_ffn_kernel, final_norm=final_norm),
        grid=(rows // tm,),
        in_specs=[
            pl.BlockSpec((tm, d_model), lambda i: (i, 0)),
            _resident((None, 1, d_model), lay(0, 0)),
            _resident((None, d_model, d_ff), lay(0, 0)),
            _resident((None, d_model, d_ff), lay(0, 0)),
            _resident((None, d_ff, d_model), lay(0, 0)),
            _resident((1, d_model), lambda i: (0, 0)),
        ],
        out_specs=pl.BlockSpec((tm, d_model), lambda i: (i, 0)),
        out_shape=jax.ShapeDtypeStruct(x2d.shape, F32),
        compiler_params=pltpu.CompilerParams(
            dimension_semantics=("arbitrary",),
            vmem_limit_bytes=FFN_VMEM_BYTES),
        name=f"ffn_l{layer}",
    )(x2d, n2g, wg, wu, wd, fg)


def kernel(x, norm1_g, w_in, dn_conv_w, dn_a_log, dn_dt_bias, dn_norm_g, sc_conv_w, sc_norm_g,
           w_out, norm2_g, ffn_w_gate, ffn_w_up, ffn_w_down, final_norm_g):
    depth, d_model, _ = w_in.shape
    bsz, seq, _ = x.shape
    sc_width = sc_norm_g.shape[-1]

    o1 = QKV_COLS + DN_WIDTH
    o2 = o1 + 2 * DN_HEADS
    small = jnp.pad(w_in[..., o1:o2], ((0, 0), (0, 0), (0, LANES - 2 * DN_HEADS)))
    win = jnp.concatenate([w_in[..., :o1], w_in[..., o2:], small], axis=-1).astype(BF16)
    wout = w_out.astype(BF16)
    wg = ffn_w_gate.astype(BF16)
    wu = ffn_w_up.astype(BF16)
    wd = ffn_w_down.astype(BF16)

    gp = jnp.zeros((depth, SUBLANES, LANES), F32)
    gp = gp.at[:, 0, DECAY_LANE:DECAY_LANE + DN_HEADS].set(dn_a_log.astype(F32))
    gp = gp.at[:, 1, DECAY_LANE:DECAY_LANE + DN_HEADS].set(dn_dt_bias.astype(F32))

    n1g = norm1_g.reshape(depth, 1, d_model)
    n2g = norm2_g.reshape(depth, 1, d_model)
    dng = dn_norm_g.reshape(depth, 1, HEAD_DIM)
    scg = sc_norm_g.reshape(depth, 1, sc_width)
    fg = final_norm_g.reshape(1, d_model)

    for layer in range(depth):
        x = _mixer_call(x, n1g, win, dn_conv_w, gp, dng, sc_conv_w, scg, wout, layer)
        x = _ffn_call(x.reshape(bsz * seq, d_model), n2g, wg, wu, wd, fg, layer,
                      final_norm=(layer == depth - 1)).reshape(bsz, seq, d_model)
    return x
```

```python
import functools

import jax
import jax.numpy as jnp
from jax import lax
from jax.experimental import pallas as pl
from jax.experimental.pallas import tpu as pltpu

F32 = jnp.float32
BF16 = jnp.bfloat16

EPS = 1e-6
LANES = 128
SUBLANES = 8
DN_HEADS = 4
HEAD_DIM = 128
DN_WIDTH = DN_HEADS * HEAD_DIM
DN_CONV = 4
SC_GROUPS = 4
SC_CONV = 3
CHUNK = 64
HEADS_CHUNK = DN_HEADS * CHUNK
QKV_COLS = 3 * DN_WIDTH
Z_OFF = QKV_COLS
SC_OFF = Z_OFF + DN_WIDTH
BETA_LANE = 0
DECAY_LANE = DN_HEADS

MIXER_TIME_TILE = 512
FFN_ROW_TILE = 512
MIXER_VMEM_BYTES = 52 * 1024 * 1024
FFN_VMEM_BYTES = 56 * 1024 * 1024


def _sigmoid(v):
    return 1.0 / (1.0 + jnp.exp(-v))


def _silu(v):
    half = 0.5 * v
    return half + half * jnp.tanh(half)


def _softplus(v):
    return jnp.maximum(v, 0.0) + jnp.log(1.0 + jnp.exp(-jnp.abs(v)))


def _split3(v):
    hi = v.astype(BF16)
    r1 = v - hi.astype(F32)
    mid = r1.astype(BF16)
    lo = (r1 - mid.astype(F32)).astype(BF16)
    return hi, mid, lo


def _bmm(a, b):
    return lax.dot_general(a.astype(BF16), b.astype(BF16), (((2,), (1,)), ((0,), (0,))),
                           preferred_element_type=F32)


def _bmm_nt(a, b):
    return lax.dot_general(a.astype(BF16), b.astype(BF16), (((2,), (2,)), ((0,), (0,))),
                           preferred_element_type=F32)


def _head_block_diag(xb, lane_head):
    zero = jnp.zeros_like(xb)
    return jnp.concatenate([jnp.where(lane_head == r, xb, zero) for r in range(DN_HEADS)], axis=1)


def _unit_lower_inverse_heads(lower, eye_all, lane_head):
    c = lower.shape[1]
    m = -lower
    p = eye_all + m
    bd = _head_block_diag(m.astype(BF16), lane_head)
    power = 1
    while 2 * power < c:
        m = _bmm(m, bd)
        bd = _head_block_diag(m.astype(BF16), lane_head)
        p = p + _bmm(p, bd)
        power *= 2
    return p


def _rms_scale(v, width):
    return lax.rsqrt(jnp.sum(v * v, axis=-1, keepdims=True) * (1.0 / width) + EPS)


def _mixer_kernel(x_ref, n1g_ref, win_ref, cw_ref, gp_ref, dng_ref, scw_ref, scg_ref, wout_ref,
                  out_ref,
                  qkv_buf, z_buf, scin_buf, small_buf, cv_buf, o_buf, s_ref,
                  bdk_buf, kbq_buf, rhs_buf, qd_buf, kdt_buf, decay_buf, glast_buf,
                  u_buf, w_buf, qkd_buf, pm_buf, qm_buf, state_buf, *, tt):
    t = pl.program_id(1)
    sc_width = scg_ref.shape[-1]
    pad = SUBLANES
    nc = tt // CHUNK
    d_model = x_ref.shape[-1]

    @pl.when(jnp.logical_and(pl.program_id(0) == 0, t == 0))
    def _():
        bdk_buf[...] = jnp.zeros_like(bdk_buf)

    @pl.when(t == 0)
    def _():
        s_ref[...] = jnp.zeros_like(s_ref)
        qkv_buf[0:pad, :] = jnp.zeros((pad, QKV_COLS), F32)
        cv_buf[0:pad, :] = jnp.zeros((pad, sc_width), F32)

    row = lax.broadcasted_iota(jnp.int32, (CHUNK, CHUNK), 0)
    col = lax.broadcasted_iota(jnp.int32, (CHUNK, CHUNK), 1)
    causal = row >= col
    tri = causal.astype(BF16)
    row_all = lax.broadcasted_iota(jnp.int32, (CHUNK, HEADS_CHUNK), 0)
    lane_all = lax.broadcasted_iota(jnp.int32, (CHUNK, HEADS_CHUNK), 1)
    lane_head = lane_all // CHUNK
    lane_in = lane_all - lane_head * CHUNK
    strict_all = row_all > lane_in
    eye_all = (row_all == lane_in).astype(F32)
    q_scale = HEAD_DIM ** -0.5
    neg_rate = -jnp.exp(gp_ref[0:1, :])
    dt_bias = gp_ref[1:2, :]
    small_off = SC_OFF + 3 * sc_width
    gw = sc_width // SC_GROUPS

    def dense_in():
        x = x_ref[...]
        hb = (x * _rms_scale(x, d_model) * n1g_ref[...]).astype(BF16)
        dot = lambda lo, hi: jnp.dot(hb, win_ref[:, lo:hi], preferred_element_type=F32)
        qkv_buf[pad:pad + tt, :] = dot(0, QKV_COLS)
        z_buf[...] = dot(Z_OFF, Z_OFF + DN_WIDTH)
        scin_buf[...] = dot(SC_OFF, small_off)
        small_buf[...] = dot(small_off, small_off + LANES)

    def stage_a(c):
        r0 = c * CHUNK
        rows = slice(r0, r0 + CHUNK)

        sm = small_buf[rows, :]
        beta_blk = _sigmoid(sm)
        g_raw = neg_rate * _softplus(sm + dt_bias)
        g_hi, g_mid, g_lo = _split3(g_raw)
        cum = lambda part: jnp.dot(tri, part, preferred_element_type=F32)
        g_blk = cum(g_hi) + (cum(g_mid) + cum(g_lo))
        g_t = g_blk.T
        eg_blk = jnp.exp(g_blk)
        g_last_blk = jnp.broadcast_to(g_blk[CHUNK - 1:CHUNK, :], (CHUNK, LANES))
        kdec_blk = jnp.exp(g_last_blk - g_blk)
        glast_buf[c] = eg_blk[CHUNK - SUBLANES:CHUNK, :]

        decays = []
        for h in range(DN_HEADS):
            def conv_silu(col0):
                first = r0 + pad - (DN_CONV - 1)
                cols = slice(col0, col0 + HEAD_DIM)
                acc = cw_ref[0:1, cols] * qkv_buf[first:first + CHUNK, cols]
                for j in range(1, DN_CONV):
                    acc = acc + cw_ref[j:j + 1, cols] * qkv_buf[first + j:first + j + CHUNK, cols]
                return _silu(acc)

            qh = conv_silu(h * HEAD_DIM)
            kh = conv_silu(DN_WIDTH + h * HEAD_DIM)
            vh = conv_silu(2 * DN_WIDTH + h * HEAD_DIM)
            qn = qh * (lax.rsqrt(jnp.sum(qh * qh, axis=-1, keepdims=True) + EPS) * q_scale)
            kn = kh * lax.rsqrt(jnp.sum(kh * kh, axis=-1, keepdims=True) + EPS)

            bl = BETA_LANE + h
            dl = DECAY_LANE + h
            beta_c = beta_blk[:, bl:bl + 1]
            g_c = g_blk[:, dl:dl + 1]
            eg_c = eg_blk[:, dl:dl + 1]
            kdec_c = kdec_blk[:, dl:dl + 1]
            g_r = g_t[dl:dl + 1, :]
            decays.append(jnp.exp(jnp.where(causal, g_c - g_r, -jnp.inf)))

            kb = kn * beta_c
            hl = slice(h * HEAD_DIM, (h + 1) * HEAD_DIM)
            bdk_buf[c, h * CHUNK:(h + 1) * CHUNK, hl] = kn.astype(BF16)
            kbq_buf[c, 0:CHUNK, hl] = kb.astype(BF16)
            kbq_buf[c, CHUNK:2 * CHUNK, hl] = qn.astype(BF16)
            rhs_buf[h, c, :, 0:HEAD_DIM] = (vh * beta_c).astype(BF16)
            rhs_buf[h, c, :, HEAD_DIM:2 * HEAD_DIM] = (kb * eg_c).astype(BF16)
            qd_buf[h, c] = (qn * eg_c).astype(BF16)
            kdt_buf[h, c] = (kn * kdec_c).T.astype(BF16)

            zh = z_buf[rows, hl]
            z_buf[rows, hl] = dng_ref[...] * _silu(zh)
        decay_buf[c] = jnp.concatenate(decays, axis=1)

        cv = scin_buf[rows, sc_width:2 * sc_width] * scin_buf[rows, 2 * sc_width:3 * sc_width]
        cv_buf[pad + r0:pad + r0 + CHUNK, :] = cv
        first = r0 + pad - (SC_CONV - 1)
        acc = scw_ref[SC_CONV - 1:SC_CONV, :] * cv
        for j in range(SC_CONV - 1):
            acc = acc + scw_ref[j:j + 1, :] * cv_buf[first + j:first + j + CHUNK, :]
        y = scin_buf[rows, 0:sc_width] * acc
        for g in range(SC_GROUPS):
            yg = y[:, g * gw:(g + 1) * gw]
            yg = yg * _rms_scale(yg, gw) * scg_ref[:, g * gw:(g + 1) * gw]
            o_buf[rows, DN_WIDTH + g * gw:DN_WIDTH + (g + 1) * gw] = yg.astype(BF16)

    def stage_b():
        kq = _bmm_nt(kbq_buf[...], bdk_buf[...])
        decay = decay_buf[...]
        lower = jnp.where(strict_all, kq[:, :CHUNK] * decay, 0.0)
        qkd_buf[...] = (kq[:, CHUNK:] * decay).astype(BF16)
        t_all = _unit_lower_inverse_heads(lower, eye_all, lane_head).astype(BF16)
        for h in range(DN_HEADS):
            sol = _bmm(t_all[:, :, h * CHUNK:(h + 1) * CHUNK], rhs_buf[h])
            u_buf[h] = sol[:, :, :HEAD_DIM]
            w_buf[h] = sol[:, :, HEAD_DIM:].astype(BF16)
            kds = _bmm(kdt_buf[h], sol)
            qm_buf[h] = kds[:, :, :HEAD_DIM]
            pm_buf[h] = (-kds[:, :, HEAD_DIM:]).astype(BF16)

    def stage_c(c):
        for h in range(DN_HEADS):
            dl = DECAY_LANE + h
            s = s_ref[h]
            sb = s.astype(BF16)
            state_buf[h, c] = sb
            gl = glast_buf[c, SUBLANES - 1:SUBLANES, dl:dl + 1]
            s_ref[h] = s * gl + (jnp.dot(pm_buf[h, c], sb, preferred_element_type=F32) + qm_buf[h, c])

    def stage_d():
        qkd = qkd_buf[...]
        for h in range(DN_HEADS):
            hl = slice(h * HEAD_DIM, (h + 1) * HEAD_DIM)
            wq = jnp.concatenate([w_buf[h], qd_buf[h]], axis=1)
            ws = _bmm(wq, state_buf[h])
            v_new = u_buf[h] - ws[:, :CHUNK]
            o = ws[:, CHUNK:] + _bmm(qkd[:, :, h * CHUNK:(h + 1) * CHUNK], v_new)
            o = o.reshape(tt, HEAD_DIM)
            o = o * _rms_scale(o, HEAD_DIM) * z_buf[:, hl]
            o_buf[:, hl] = o.astype(BF16)
        out_ref[...] = x_ref[...] + jnp.dot(o_buf[...], wout_ref[...], preferred_element_type=F32)

    dense_in()
    for c in range(nc):
        stage_a(c)
    stage_b()
    for c in range(nc):
        stage_c(c)
    stage_d()

    qkv_buf[0:pad, :] = qkv_buf[tt:tt + pad, :]
    cv_buf[0:pad, :] = cv_buf[tt:tt + pad, :]


def _ffn_kernel(x_ref, n2g_ref, wg_ref, wu_ref, wd_ref, fg_ref, out_ref, *, final_norm):
    x = x_ref[...]
    d_model = x.shape[-1]
    hb = (x * _rms_scale(x, d_model) * n2g_ref[...]).astype(BF16)
    gate = jnp.dot(hb, wg_ref[...], preferred_element_type=F32)
    up = jnp.dot(hb, wu_ref[...], preferred_element_type=F32)
    act = (_silu(gate) * up).astype(BF16)
    y = x + jnp.dot(act, wd_ref[...], preferred_element_type=F32)
    if final_norm:
        y = y * _rms_scale(y, d_model) * fg_ref[...]
    out_ref[...] = y


def _resident(shape, index_map):
    return pl.BlockSpec(shape, index_map, pipeline_mode=pl.Buffered(1))


def _mixer_call(x, n1g, win, cw, gp, dng, scw, scg, wout, layer):
    bsz, seq, d_model = x.shape
    tt = min(MIXER_TIME_TILE, seq)
    assert seq % tt == 0 and tt % CHUNK == 0
    sc_width = scg.shape[-1]
    in_cols = win.shape[-1]
    pad = SUBLANES
    nc = tt // CHUNK
    lay = lambda *rest: (lambda b, t: (layer, *rest))
    return pl.pallas_call(
        functools.partial(_mixer_kernel, tt=tt),
        grid=(bsz, seq // tt),
        in_specs=[
            pl.BlockSpec((None, tt, d_model), lambda b, t: (b, t, 0)),
            _resident((None, 1, d_model), lay(0, 0)),
            _resident((None, d_model, in_cols), lay(0, 0)),
            _resident((None, DN_CONV, QKV_COLS), lay(0, 0)),
            _resident((None, SUBLANES, LANES), lay(0, 0)),
            _resident((None, 1, HEAD_DIM), lay(0, 0)),
            _resident((None, SC_CONV, sc_width), lay(0, 0)),
            _resident((None, 1, sc_width), lay(0, 0)),
            _resident((None, d_model, d_model), lay(0, 0)),
        ],
        out_specs=pl.BlockSpec((None, tt, d_model), lambda b, t: (b, t, 0)),
        out_shape=jax.ShapeDtypeStruct(x.shape, F32),
        scratch_shapes=[
            pltpu.VMEM((tt + pad, QKV_COLS), F32),
            pltpu.VMEM((tt, DN_WIDTH), F32),
            pltpu.VMEM((tt, 3 * sc_width), F32),
            pltpu.VMEM((tt, LANES), F32),
            pltpu.VMEM((tt + pad, sc_width), F32),
            pltpu.VMEM((tt, d_model), BF16),
            pltpu.VMEM((DN_HEADS, HEAD_DIM, HEAD_DIM), F32),
            pltpu.VMEM((nc, HEADS_CHUNK, DN_WIDTH), BF16),
            pltpu.VMEM((nc, 2 * CHUNK, DN_WIDTH), BF16),
            pltpu.VMEM((DN_HEADS, nc, CHUNK, 2 * HEAD_DIM), BF16),
            pltpu.VMEM((DN_HEADS, nc, CHUNK, HEAD_DIM), BF16),
            pltpu.VMEM((DN_HEADS, nc, HEAD_DIM, CHUNK), BF16),
            pltpu.VMEM((nc, CHUNK, HEADS_CHUNK), F32),
            pltpu.VMEM((nc, SUBLANES, LANES), F32),
            pltpu.VMEM((DN_HEADS, nc, CHUNK, HEAD_DIM), F32),
            pltpu.VMEM((DN_HEADS, nc, CHUNK, HEAD_DIM), BF16),
            pltpu.VMEM((nc, CHUNK, HEADS_CHUNK), BF16),
            pltpu.VMEM((DN_HEADS, nc, HEAD_DIM, HEAD_DIM), BF16),
            pltpu.VMEM((DN_HEADS, nc, HEAD_DIM, HEAD_DIM), F32),
            pltpu.VMEM((DN_HEADS, nc, HEAD_DIM, HEAD_DIM), BF16),
        ],
        compiler_params=pltpu.CompilerParams(
            dimension_semantics=("arbitrary", "arbitrary"),
            vmem_limit_bytes=MIXER_VMEM_BYTES),
        name=f"mixer_l{layer}",
    )(x, n1g, win, cw, gp, dng, scw, scg, wout)


def _ffn_call(x2d, n2g, wg, wu, wd, fg, layer, final_norm):
    rows, d_model = x2d.shape
    tm = min(FFN_ROW_TILE, rows)
    assert rows % tm == 0
    d_ff = wg.shape[-1]
    lay = lambda *rest: (lambda i: (layer, *rest))
    return pl.pallas_call(
        functools.partial(_ffn_kernel, final_norm=final_norm),
        grid=(rows // tm,),
        in_specs=[
            pl.BlockSpec((tm, d_model), lambda i: (i, 0)),
            _resident((None, 1, d_model), lay(0, 0)),
            _resident((None, d_model, d_ff), lay(0, 0)),
            _resident((None, d_model, d_ff), lay(0, 0)),
            _resident((None, d_ff, d_model), lay(0, 0)),
            _resident((1, d_model), lambda i: (0, 0)),
        ],
        out_specs=pl.BlockSpec((tm, d_model), lambda i: (i, 0)),
        out_shape=jax.ShapeDtypeStruct(x2d.shape, F32),
        compiler_params=pltpu.CompilerParams(
            dimension_semantics=("arbitrary",),
            vmem_limit_bytes=FFN_VMEM_BYTES),
        name=f"ffn_l{layer}",
    )(x2d, n2g, wg, wu, wd, fg)


def kernel(x, norm1_g, w_in, dn_conv_w, dn_a_log, dn_dt_bias, dn_norm_g, sc_conv_w, sc_norm_g,
           w_out, norm2_g, ffn_w_gate, ffn_w_up, ffn_w_down, final_norm_g):
    depth, d_model, _ = w_in.shape
    bsz, seq, _ = x.shape
    sc_width = sc_norm_g.shape[-1]

    o1 = QKV_COLS + DN_WIDTH
    o2 = o1 + 2 * DN_HEADS
    small = jnp.pad(w_in[..., o1:o2], ((0, 0), (0, 0), (0, LANES - 2 * DN_HEADS)))
    win = jnp.concatenate([w_in[..., :o1], w_in[..., o2:], small], axis=-1).astype(BF16)
    wout = w_out.astype(BF16)
    wg = ffn_w_gate.astype(BF16)
    wu = ffn_w_up.astype(BF16)
    wd = ffn_w_down.astype(BF16)

    gp = jnp.zeros((depth, SUBLANES, LANES), F32)
    gp = gp.at[:, 0, DECAY_LANE:DECAY_LANE + DN_HEADS].set(dn_a_log.astype(F32))
    gp = gp.at[:, 1, DECAY_LANE:DECAY_LANE + DN_HEADS].set(dn_dt_bias.astype(F32))

    n1g = norm1_g.reshape(depth, 1, d_model)
    n2g = norm2_g.reshape(depth, 1, d_model)
    dng = dn_norm_g.reshape(depth, 1, HEAD_DIM)
    scg = sc_norm_g.reshape(depth, 1, sc_width)
    fg = final_norm_g.reshape(1, d_model)

    for layer in range(depth):
        x = _mixer_call(x, n1g, win, dn_conv_w, gp, dng, sc_conv_w, scg, wout, layer)
        x = _ffn_call(x.reshape(bsz * seq, d_model), n2g, wg, wu, wd, fg, layer,
                      final_norm=(layer == depth - 1)).reshape(bsz, seq, d_model)
    return x
```

```python
import functools

import jax
import jax.numpy as jnp
from jax import lax
from jax.experimental import pallas as pl
from jax.experimental.pallas import tpu as pltpu

F32 = jnp.float32
BF16 = jnp.bfloat16

EPS = 1e-6
LANES = 128
SUBLANES = 8
DN_HEADS = 4
HEAD_DIM = 128
DN_WIDTH = DN_HEADS * HEAD_DIM
DN_CONV = 4
SC_GROUPS = 4
SC_CONV = 3
CHUNK = 64
HEADS_CHUNK = DN_HEADS * CHUNK
QKV_COLS = 3 * DN_WIDTH
Z_OFF = QKV_COLS
SC_OFF = Z_OFF + DN_WIDTH
BETA_LANE = 0
DECAY_LANE = DN_HEADS

MIXER_TIME_TILE = 512
FFN_ROW_TILE = 512
MIXER_VMEM_BYTES = 52 * 1024 * 1024
FFN_VMEM_BYTES = 56 * 1024 * 1024


def _sigmoid(v):
    return 1.0 / (1.0 + jnp.exp(-v))


def _silu(v):
    half = 0.5 * v
    return half + half * jnp.tanh(half)


def _softplus(v):
    return jnp.maximum(v, 0.0) + jnp.log(1.0 + jnp.exp(-jnp.abs(v)))


def _split3(v):
    hi = v.astype(BF16)
    r1 = v - hi.astype(F32)
    mid = r1.astype(BF16)
    lo = (r1 - mid.astype(F32)).astype(BF16)
    return hi, mid, lo


def _bmm(a, b):
    return lax.dot_general(a.astype(BF16), b.astype(BF16), (((2,), (1,)), ((0,), (0,))),
                           preferred_element_type=F32)


def _bmm_nt(a, b):
    return lax.dot_general(a.astype(BF16), b.astype(BF16), (((2,), (2,)), ((0,), (0,))),
                           preferred_element_type=F32)


def _head_block_diag(xb, lane_head):
    zero = jnp.zeros_like(xb)
    return jnp.concatenate([jnp.where(lane_head == r, xb, zero) for r in range(DN_HEADS)], axis=1)


def _unit_lower_inverse_heads(lower, row_all, lane_in, lane_head):
    c = lower.shape[1]

    def off_diag(s):
        same_block = (row_all & -(2 * s)) == (lane_in & -(2 * s))
        return same_block & ((row_all & s) != 0) & ((lane_in & s) == 0)

    t = (row_all == lane_in).astype(F32) - jnp.where(off_diag(1), lower, 0.0)
    s = 2
    while s < c:
        x = _bmm(jnp.where(off_diag(s), lower, 0.0), _head_block_diag(t.astype(BF16), lane_head))
        t = t - _bmm(t, _head_block_diag(x.astype(BF16), lane_head))
        s *= 2
    return t


def _rms_scale(v, width):
    return lax.rsqrt(jnp.sum(v * v, axis=-1, keepdims=True) * (1.0 / width) + EPS)


def _mixer_kernel(x_ref, n1g_ref, win_ref, cw_ref, gp_ref, dng_ref, scw_ref, scg_ref, wout_ref,
                  out_ref,
                  qkv_buf, z_buf, scin_buf, small_buf, cv_buf, o_buf, s_ref,
                  bdk_buf, kbq_buf, rhs_buf, qd_buf, kdt_buf, decay_buf, glast_buf,
                  u_buf, w_buf, qkd_buf, pm_buf, qm_buf, state_buf, *, tt):
    t = pl.program_id(1)
    sc_width = scg_ref.shape[-1]
    pad = SUBLANES
    nc = tt // CHUNK
    d_model = x_ref.shape[-1]

    @pl.when(jnp.logical_and(pl.program_id(0) == 0, t == 0))
    def _():
        bdk_buf[...] = jnp.zeros_like(bdk_buf)

    @pl.when(t == 0)
    def _():
        s_ref[...] = jnp.zeros_like(s_ref)
        qkv_buf[0:pad, :] = jnp.zeros((pad, QKV_COLS), F32)
        cv_buf[0:pad, :] = jnp.zeros((pad, sc_width), F32)

    row = lax.broadcasted_iota(jnp.int32, (CHUNK, CHUNK), 0)
    col = lax.broadcasted_iota(jnp.int32, (CHUNK, CHUNK), 1)
    causal = row >= col
    tri = causal.astype(BF16)
    row_all = lax.broadcasted_iota(jnp.int32, (CHUNK, HEADS_CHUNK), 0)
    lane_all = lax.broadcasted_iota(jnp.int32, (CHUNK, HEADS_CHUNK), 1)
    lane_head = lane_all // CHUNK
    lane_in = lane_all - lane_head * CHUNK
    strict_all = row_all > lane_in
    q_scale = HEAD_DIM ** -0.5
    neg_rate = -jnp.exp(gp_ref[0:1, :])
    dt_bias = gp_ref[1:2, :]
    small_off = SC_OFF + 3 * sc_width
    gw = sc_width // SC_GROUPS

    def dense_in():
        x = x_ref[...]
        hb = (x * _rms_scale(x, d_model) * n1g_ref[...]).astype(BF16)
        dot = lambda lo, hi: jnp.dot(hb, win_ref[:, lo:hi], preferred_element_type=F32)
        qkv_buf[pad:pad + tt, :] = dot(0, QKV_COLS)
        z_buf[...] = dot(Z_OFF, Z_OFF + DN_WIDTH)
        scin_buf[...] = dot(SC_OFF, small_off)
        small_buf[...] = dot(small_off, small_off + LANES)

    def stage_a(c):
        r0 = c * CHUNK
        rows = slice(r0, r0 + CHUNK)

        sm = small_buf[rows, :]
        beta_blk = _sigmoid(sm)
        g_raw = neg_rate * _softplus(sm + dt_bias)
        g_hi, g_mid, g_lo = _split3(g_raw)
        cum = lambda part: jnp.dot(tri, part, preferred_element_type=F32)
        g_blk = cum(g_hi) + (cum(g_mid) + cum(g_lo))
        g_t = g_blk.T
        eg_blk = jnp.exp(g_blk)
        g_last_blk = jnp.broadcast_to(g_blk[CHUNK - 1:CHUNK, :], (CHUNK, LANES))
        kdec_blk = jnp.exp(g_last_blk - g_blk)
        glast_buf[c] = eg_blk[CHUNK - SUBLANES:CHUNK, :]

        decays = []
        for h in range(DN_HEADS):
            def conv_silu(col0):
                first = r0 + pad - (DN_CONV - 1)
                cols = slice(col0, col0 + HEAD_DIM)
                acc = cw_ref[0:1, cols] * qkv_buf[first:first + CHUNK, cols]
                for j in range(1, DN_CONV):
                    acc = acc + cw_ref[j:j + 1, cols] * qkv_buf[first + j:first + j + CHUNK, cols]
                return _silu(acc)

            qh = conv_silu(h * HEAD_DIM)
            kh = conv_silu(DN_WIDTH + h * HEAD_DIM)
            vh = conv_silu(2 * DN_WIDTH + h * HEAD_DIM)
            qn = qh * (lax.rsqrt(jnp.sum(qh * qh, axis=-1, keepdims=True) + EPS) * q_scale)
            kn = kh * lax.rsqrt(jnp.sum(kh * kh, axis=-1, keepdims=True) + EPS)

            bl = BETA_LANE + h
            dl = DECAY_LANE + h
            beta_c = beta_blk[:, bl:bl + 1]
            g_c = g_blk[:, dl:dl + 1]
            eg_c = eg_blk[:, dl:dl + 1]
            kdec_c = kdec_blk[:, dl:dl + 1]
            g_r = g_t[dl:dl + 1, :]
            decays.append(jnp.exp(jnp.where(causal, g_c - g_r, -jnp.inf)))

            kb = kn * beta_c
            hl = slice(h * HEAD_DIM, (h + 1) * HEAD_DIM)
            bdk_buf[c, h * CHUNK:(h + 1) * CHUNK, hl] = kn.astype(BF16)
            kbq_buf[c, 0:CHUNK, hl] = kb.astype(BF16)
            kbq_buf[c, CHUNK:2 * CHUNK, hl] = qn.astype(BF16)
            rhs_buf[h, c, :, 0:HEAD_DIM] = (vh * beta_c).astype(BF16)
            rhs_buf[h, c, :, HEAD_DIM:2 * HEAD_DIM] = (kb * eg_c).astype(BF16)
            qd_buf[h, c] = (qn * eg_c).astype(BF16)
            kdt_buf[h, c] = (kn * kdec_c).T.astype(BF16)

            zh = z_buf[rows, hl]
            z_buf[rows, hl] = dng_ref[...] * _silu(zh)
        decay_buf[c] = jnp.concatenate(decays, axis=1)

        cv = scin_buf[rows, sc_width:2 * sc_width] * scin_buf[rows, 2 * sc_width:3 * sc_width]
        cv_buf[pad + r0:pad + r0 + CHUNK, :] = cv
        first = r0 + pad - (SC_CONV - 1)
        acc = scw_ref[SC_CONV - 1:SC_CONV, :] * cv
        for j in range(SC_CONV - 1):
            acc = acc + scw_ref[j:j + 1, :] * cv_buf[first + j:first + j + CHUNK, :]
        y = scin_buf[rows, 0:sc_width] * acc
        for g in range(SC_GROUPS):
            yg = y[:, g * gw:(g + 1) * gw]
            yg = yg * _rms_scale(yg, gw) * scg_ref[:, g * gw:(g + 1) * gw]
            o_buf[rows, DN_WIDTH + g * gw:DN_WIDTH + (g + 1) * gw] = yg.astype(BF16)

    def stage_b():
        kq = _bmm_nt(kbq_buf[...], bdk_buf[...])
        decay = decay_buf[...]
        lower = jnp.where(strict_all, kq[:, :CHUNK] * decay, 0.0)
        qkd_buf[...] = (kq[:, CHUNK:] * decay).astype(BF16)
        t_all = _unit_lower_inverse_heads(lower, row_all, lane_in, lane_head).astype(BF16)
        for h in range(DN_HEADS):
            sol = _bmm(t_all[:, :, h * CHUNK:(h + 1) * CHUNK], rhs_buf[h])
            u_buf[h] = sol[:, :, :HEAD_DIM]
            w_buf[h] = sol[:, :, HEAD_DIM:].astype(BF16)
            kds = _bmm(kdt_buf[h], sol)
            qm_buf[h] = kds[:, :, :HEAD_DIM]
            pm_buf[h] = (-kds[:, :, HEAD_DIM:]).astype(BF16)

    def stage_c(c):
        for h in range(DN_HEADS):
            dl = DECAY_LANE + h
            s = s_ref[h]
            sb = s.astype(BF16)
            state_buf[h, c] = sb
            gl = glast_buf[c, SUBLANES - 1:SUBLANES, dl:dl + 1]
            s_ref[h] = s * gl + (jnp.dot(pm_buf[h, c], sb, preferred_element_type=F32) + qm_buf[h, c])

    def stage_d():
        qkd = qkd_buf[...]
        for h in range(DN_HEADS):
            hl = slice(h * HEAD_DIM, (h + 1) * HEAD_DIM)
            wq = jnp.concatenate([w_buf[h], qd_buf[h]], axis=1)
            ws = _bmm(wq, state_buf[h])
            v_new = u_buf[h] - ws[:, :CHUNK]
            o = ws[:, CHUNK:] + _bmm(qkd[:, :, h * CHUNK:(h + 1) * CHUNK], v_new)
            o = o.reshape(tt, HEAD_DIM)
            o = o * _rms_scale(o, HEAD_DIM) * z_buf[:, hl]
            o_buf[:, hl] = o.astype(BF16)
        out_ref[...] = x_ref[...] + jnp.dot(o_buf[...], wout_ref[...], preferred_element_type=F32)

    dense_in()
    for c in range(nc):
        stage_a(c)
    stage_b()
    for c in range(nc):
        stage_c(c)
    stage_d()

    qkv_buf[0:pad, :] = qkv_buf[tt:tt + pad, :]
    cv_buf[0:pad, :] = cv_buf[tt:tt + pad, :]


def _ffn_kernel(x_ref, n2g_ref, wg_ref, wu_ref, wd_ref, fg_ref, out_ref, *, final_norm):
    x = x_ref[...]
    d_model = x.shape[-1]
    hb = (x * _rms_scale(x, d_model) * n2g_ref[...]).astype(BF16)
    gate = jnp.dot(hb, wg_ref[...], preferred_element_type=F32)
    up = jnp.dot(hb, wu_ref[...], preferred_element_type=F32)
    act = (_silu(gate) * up).astype(BF16)
    y = x + jnp.dot(act, wd_ref[...], preferred_element_type=F32)
    if final_norm:
        y = y * _rms_scale(y, d_model) * fg_ref[...]
    out_ref[...] = y


def _resident(shape, index_map):
    return pl.BlockSpec(shape, index_map, pipeline_mode=pl.Buffered(1))


def _mixer_call(x, n1g, win, cw, gp, dng, scw, scg, wout, layer):
    bsz, seq, d_model = x.shape
    tt = min(MIXER_TIME_TILE, seq)
    assert seq % tt == 0 and tt % CHUNK == 0
    sc_width = scg.shape[-1]
    in_cols = win.shape[-1]
    pad = SUBLANES
    nc = tt // CHUNK
    lay = lambda *rest: (lambda b, t: (layer, *rest))
    return pl.pallas_call(
        functools.partial(_mixer_kernel, tt=tt),
        grid=(bsz, seq // tt),
        in_specs=[
            pl.BlockSpec((None, tt, d_model), lambda b, t: (b, t, 0)),
            _resident((None, 1, d_model), lay(0, 0)),
            _resident((None, d_model, in_cols), lay(0, 0)),
            _resident((None, DN_CONV, QKV_COLS), lay(0, 0)),
            _resident((None, SUBLANES, LANES), lay(0, 0)),
            _resident((None, 1, HEAD_DIM), lay(0, 0)),
            _resident((None, SC_CONV, sc_width), lay(0, 0)),
            _resident((None, 1, sc_width), lay(0, 0)),
            _resident((None, d_model, d_model), lay(0, 0)),
        ],
        out_specs=pl.BlockSpec((None, tt, d_model), lambda b, t: (b, t, 0)),
        out_shape=jax.ShapeDtypeStruct(x.shape, F32),
        scratch_shapes=[
            pltpu.VMEM((tt + pad, QKV_COLS), F32),
            pltpu.VMEM((tt, DN_WIDTH), F32),
            pltpu.VMEM((tt, 3 * sc_width), F32),
            pltpu.VMEM((tt, LANES), F32),
            pltpu.VMEM((tt + pad, sc_width), F32),
            pltpu.VMEM((tt, d_model), BF16),
            pltpu.VMEM((DN_HEADS, HEAD_DIM, HEAD_DIM), F32),
            pltpu.VMEM((nc, HEADS_CHUNK, DN_WIDTH), BF16),
            pltpu.VMEM((nc, 2 * CHUNK, DN_WIDTH), BF16),
            pltpu.VMEM((DN_HEADS, nc, CHUNK, 2 * HEAD_DIM), BF16),
            pltpu.VMEM((DN_HEADS, nc, CHUNK, HEAD_DIM), BF16),
            pltpu.VMEM((DN_HEADS, nc, HEAD_DIM, CHUNK), BF16),
            pltpu.VMEM((nc, CHUNK, HEADS_CHUNK), F32),
            pltpu.VMEM((nc, SUBLANES, LANES), F32),
            pltpu.VMEM((DN_HEADS, nc, CHUNK, HEAD_DIM), F32),
            pltpu.VMEM((DN_HEADS, nc, CHUNK, HEAD_DIM), BF16),
            pltpu.VMEM((nc, CHUNK, HEADS_CHUNK), BF16),
            pltpu.VMEM((DN_HEADS, nc, HEAD_DIM, HEAD_DIM), BF16),
            pltpu.VMEM((DN_HEADS, nc, HEAD_DIM, HEAD_DIM), F32),
            pltpu.VMEM((DN_HEADS, nc, HEAD_DIM, HEAD_DIM), BF16),
        ],
        compiler_params=pltpu.CompilerParams(
            dimension_semantics=("arbitrary", "arbitrary"),
            vmem_limit_bytes=MIXER_VMEM_BYTES),
        name=f"mixer_l{layer}",
    )(x, n1g, win, cw, gp, dng, scw, scg, wout)


def _ffn_call(x2d, n2g, wg, wu, wd, fg, layer, final_norm):
    rows, d_model = x2d.shape
    tm = min(FFN_ROW_TILE, rows)
    assert rows % tm == 0
    d_ff = wg.shape[-1]
    lay = lambda *rest: (lambda i: (layer, *rest))
    return pl.pallas_call(
        functools.partial(_ffn_kernel, final_norm=final_norm),
        grid=(rows // tm,),
        in_specs=[
            pl.BlockSpec((tm, d_model), lambda i: (i, 0)),
            _resident((None, 1, d_model), lay(0, 0)),
            _resident((None, d_model, d_ff), lay(0, 0)),
            _resident((None, d_model, d_ff), lay(0, 0)),
            _resident((None, d_ff, d_model), lay(0, 0)),
            _resident((1, d_model), lambda i: (0, 0)),
        ],
        out_specs=pl.BlockSpec((tm, d_model), lambda i: (i, 0)),
        out_shape=jax.ShapeDtypeStruct(x2d.shape, F32),
        compiler_params=pltpu.CompilerParams(
            dimension_semantics=("arbitrary",),
            vmem_limit_bytes=FFN_VMEM_BYTES),
        name=f"ffn_l{layer}",
    )(x2d, n2g, wg, wu, wd, fg)


def kernel(x, norm1_g, w_in, dn_conv_w, dn_a_log, dn_dt_bias, dn_norm_g, sc_conv_w, sc_norm_g,
           w_out, norm2_g, ffn_w_gate, ffn_w_up, ffn_w_down, final_norm_g):
    depth, d_model, _ = w_in.shape
    bsz, seq, _ = x.shape
    sc_width = sc_norm_g.shape[-1]

    o1 = QKV_COLS + DN_WIDTH
    o2 = o1 + 2 * DN_HEADS
    small = jnp.pad(w_in[..., o1:o2], ((0, 0), (0, 0), (0, LANES - 2 * DN_HEADS)))
    win = jnp.concatenate([w_in[..., :o1], w_in[..., o2:], small], axis=-1).astype(BF16)
    wout = w_out.astype(BF16)
    wg = ffn_w_gate.astype(BF16)
    wu = ffn_w_up.astype(BF16)
    wd = ffn_w_down.astype(BF16)

    gp = jnp.zeros((depth, SUBLANES, LANES), F32)
    gp = gp.at[:, 0, DECAY_LANE:DECAY_LANE + DN_HEADS].set(dn_a_log.astype(F32))
    gp = gp.at[:, 1, DECAY_LANE:DECAY_LANE + DN_HEADS].set(dn_dt_bias.astype(F32))

    n1g = norm1_g.reshape(depth, 1, d_model)
    n2g = norm2_g.reshape(depth, 1, d_model)
    dng = dn_norm_g.reshape(depth, 1, HEAD_DIM)
    scg = sc_norm_g.reshape(depth, 1, sc_width)
    fg = final_norm_g.reshape(1, d_model)

    for layer in range(depth):
        x = _mixer_call(x, n1g, win, dn_conv_w, gp, dng, sc_conv_w, scg, wout, layer)
        x = _ffn_call(x.reshape(bsz * seq, d_model), n2g, wg, wu, wd, fg, layer,
                      final_norm=(layer == depth - 1)).reshape(bsz, seq, d_model)
    return x
```

```python
import functools

import jax
import jax.numpy as jnp
from jax import lax
from jax.experimental import pallas as pl
from jax.experimental.pallas import tpu as pltpu

F32 = jnp.float32
BF16 = jnp.bfloat16

EPS = 1e-6
LANES = 128
SUBLANES = 8
DN_HEADS = 4
HEAD_DIM = 128
DN_WIDTH = DN_HEADS * HEAD_DIM
DN_CONV = 4
SC_GROUPS = 4
SC_CONV = 3
CHUNK = 64
HEADS_CHUNK = DN_HEADS * CHUNK
QKV_COLS = 3 * DN_WIDTH
Z_OFF = QKV_COLS
SC_OFF = Z_OFF + DN_WIDTH
BETA_LANE = 0
DECAY_LANE = DN_HEADS

MIXER_TIME_TILE = 512
FFN_ROW_TILE = 512
MIXER_VMEM_BYTES = 52 * 1024 * 1024
FFN_VMEM_BYTES = 56 * 1024 * 1024


def _sigmoid(v):
    return 1.0 / (1.0 + jnp.exp(-v))


def _silu(v):
    half = 0.5 * v
    return half + half * jnp.tanh(half)


def _softplus(v):
    return jnp.maximum(v, 0.0) + jnp.log(1.0 + jnp.exp(-jnp.abs(v)))


def _split3(v):
    hi = v.astype(BF16)
    r1 = v - hi.astype(F32)
    mid = r1.astype(BF16)
    lo = (r1 - mid.astype(F32)).astype(BF16)
    return hi, mid, lo


def _bmm(a, b):
    return lax.dot_general(a.astype(BF16), b.astype(BF16), (((2,), (1,)), ((0,), (0,))),
                           preferred_element_type=F32)


def _bmm_nt(a, b):
    return lax.dot_general(a.astype(BF16), b.astype(BF16), (((2,), (2,)), ((0,), (0,))),
                           preferred_element_type=F32)


def _head_block_diag(xb, lane_head):
    zero = jnp.zeros_like(xb)
    return jnp.concatenate([jnp.where(lane_head == r, xb, zero) for r in range(DN_HEADS)], axis=1)


def _unit_lower_inverse_heads(lower, row_all, lane_in, lane_head):
    c = lower.shape[1]

    def off_diag(s):
        same_block = (row_all & -(2 * s)) == (lane_in & -(2 * s))
        return same_block & ((row_all & s) != 0) & ((lane_in & s) == 0)

    t = (row_all == lane_in).astype(F32) - jnp.where(off_diag(1), lower, 0.0)
    s = 2
    while s < c:
        x = _bmm(jnp.where(off_diag(s), lower, 0.0), _head_block_diag(t.astype(BF16), lane_head))
        t = t - _bmm(t, _head_block_diag(x.astype(BF16), lane_head))
        s *= 2
    return t


def _rms_scale(v, width):
    return lax.rsqrt(jnp.sum(v * v, axis=-1, keepdims=True) * (1.0 / width) + EPS)


def _mixer_kernel(x_ref, n1g_ref, win_ref, cw_ref, gp_ref, dng_ref, scw_ref, scg_ref, wout_ref,
                  out_ref,
                  qkv_buf, z_buf, scin_buf, small_buf, cv_buf, o_buf, s_ref,
                  bdk_buf, kbq_buf, rhs_buf, qd_buf, kdt_buf, decay_buf, glast_buf,
                  u_buf, w_buf, qkd_buf, pm_buf, qm_buf, state_buf, *, tt):
    t = pl.program_id(1)
    sc_width = scg_ref.shape[-1]
    pad = SUBLANES
    nc = tt // CHUNK
    d_model = x_ref.shape[-1]

    @pl.when(jnp.logical_and(pl.program_id(0) == 0, t == 0))
    def _():
        bdk_buf[...] = jnp.zeros_like(bdk_buf)

    @pl.when(t == 0)
    def _():
        s_ref[...] = jnp.zeros_like(s_ref)
        qkv_buf[0:pad, :] = jnp.zeros((pad, QKV_COLS), F32)
        cv_buf[0:pad, :] = jnp.zeros((pad, sc_width), F32)

    row = lax.broadcasted_iota(jnp.int32, (CHUNK, CHUNK), 0)
    col = lax.broadcasted_iota(jnp.int32, (CHUNK, CHUNK), 1)
    causal = row >= col
    tri = causal.astype(BF16)
    row_all = lax.broadcasted_iota(jnp.int32, (CHUNK, HEADS_CHUNK), 0)
    lane_all = lax.broadcasted_iota(jnp.int32, (CHUNK, HEADS_CHUNK), 1)
    lane_head = lane_all // CHUNK
    lane_in = lane_all - lane_head * CHUNK
    strict_all = row_all > lane_in
    q_scale = HEAD_DIM ** -0.5
    neg_rate = -jnp.exp(gp_ref[0:1, :])
    dt_bias = gp_ref[1:2, :]
    small_off = SC_OFF + 3 * sc_width
    gw = sc_width // SC_GROUPS

    def dense_in():
        x = x_ref[...]
        hb = (x * _rms_scale(x, d_model) * n1g_ref[...]).astype(BF16)
        dot = lambda lo, hi: jnp.dot(hb, win_ref[:, lo:hi], preferred_element_type=F32)
        qkv_buf[pad:pad + tt, :] = dot(0, QKV_COLS)
        z_buf[...] = dot(Z_OFF, Z_OFF + DN_WIDTH)
        scin_buf[...] = dot(SC_OFF, small_off)
        small_buf[...] = dot(small_off, small_off + LANES)

    def stage_a(c):
        r0 = c * CHUNK
        rows = slice(r0, r0 + CHUNK)

        sm = small_buf[rows, :]
        beta_blk = _sigmoid(sm)
        g_raw = neg_rate * _softplus(sm + dt_bias)
        g_hi, g_mid, g_lo = _split3(g_raw)
        cum = lambda part: jnp.dot(tri, part, preferred_element_type=F32)
        g_blk = cum(g_hi) + (cum(g_mid) + cum(g_lo))
        g_t = g_blk.T
        eg_blk = jnp.exp(g_blk)
        g_last_blk = jnp.broadcast_to(g_blk[CHUNK - 1:CHUNK, :], (CHUNK, LANES))
        kdec_blk = jnp.exp(g_last_blk - g_blk)
        glast_buf[c] = eg_blk[CHUNK - SUBLANES:CHUNK, :]

        decays = []
        for h in range(DN_HEADS):
            def conv_silu(col0):
                first = r0 + pad - (DN_CONV - 1)
                cols = slice(col0, col0 + HEAD_DIM)
                last = DN_CONV - 1
                acc = cw_ref[last:last + 1, cols] * qkv_buf[first + last:first + last + CHUNK, cols]
                for j in range(last):
                    acc = acc + cw_ref[j:j + 1, cols] * qkv_buf[first + j:first + j + CHUNK, cols]
                return _silu(acc)

            qh = conv_silu(h * HEAD_DIM)
            kh = conv_silu(DN_WIDTH + h * HEAD_DIM)
            vh = conv_silu(2 * DN_WIDTH + h * HEAD_DIM)
            qn = qh * (lax.rsqrt(jnp.sum(qh * qh, axis=-1, keepdims=True) + EPS) * q_scale)
            kn = kh * lax.rsqrt(jnp.sum(kh * kh, axis=-1, keepdims=True) + EPS)

            bl = BETA_LANE + h
            dl = DECAY_LANE + h
            beta_c = beta_blk[:, bl:bl + 1]
            g_c = g_blk[:, dl:dl + 1]
            eg_c = eg_blk[:, dl:dl + 1]
            kdec_c = kdec_blk[:, dl:dl + 1]
            g_r = g_t[dl:dl + 1, :]
            decays.append(jnp.exp(jnp.where(causal, g_c - g_r, -jnp.inf)))

            kb = kn * beta_c
            hl = slice(h * HEAD_DIM, (h + 1) * HEAD_DIM)
            bdk_buf[c, h * CHUNK:(h + 1) * CHUNK, hl] = kn.astype(BF16)
            kbq_buf[c, 0:CHUNK, hl] = kb.astype(BF16)
            kbq_buf[c, CHUNK:2 * CHUNK, hl] = qn.astype(BF16)
            rhs_buf[h, c, :, 0:HEAD_DIM] = (vh * beta_c).astype(BF16)
            rhs_buf[h, c, :, HEAD_DIM:2 * HEAD_DIM] = (kb * eg_c).astype(BF16)
            qd_buf[h, c] = (qn * eg_c).astype(BF16)
            kdt_buf[h, c] = (kn * kdec_c).T.astype(BF16)

            zh = z_buf[rows, hl]
            z_buf[rows, hl] = dng_ref[...] * _silu(zh)
        decay_buf[c] = jnp.concatenate(decays, axis=1)

        cv = scin_buf[rows, sc_width:2 * sc_width] * scin_buf[rows, 2 * sc_width:3 * sc_width]
        cv_buf[pad + r0:pad + r0 + CHUNK, :] = cv
        first = r0 + pad - (SC_CONV - 1)
        acc = scw_ref[SC_CONV - 1:SC_CONV, :] * cv
        for j in range(SC_CONV - 1):
            acc = acc + scw_ref[j:j + 1, :] * cv_buf[first + j:first + j + CHUNK, :]
        y = scin_buf[rows, 0:sc_width] * acc
        for g in range(SC_GROUPS):
            yg = y[:, g * gw:(g + 1) * gw]
            yg = yg * _rms_scale(yg, gw) * scg_ref[:, g * gw:(g + 1) * gw]
            o_buf[rows, DN_WIDTH + g * gw:DN_WIDTH + (g + 1) * gw] = yg.astype(BF16)

    def stage_b():
        kq = _bmm_nt(kbq_buf[...], bdk_buf[...])
        decay = decay_buf[...]
        lower = jnp.where(strict_all, kq[:, :CHUNK] * decay, 0.0)
        qkd_buf[...] = (kq[:, CHUNK:] * decay).astype(BF16)
        t_all = _unit_lower_inverse_heads(lower, row_all, lane_in, lane_head).astype(BF16)
        for h in range(DN_HEADS):
            sol = _bmm(t_all[:, :, h * CHUNK:(h + 1) * CHUNK], rhs_buf[h])
            u_buf[h] = sol[:, :, :HEAD_DIM]
            w_buf[h] = sol[:, :, HEAD_DIM:].astype(BF16)
            kds = _bmm(kdt_buf[h], sol)
            qm_buf[h] = kds[:, :, :HEAD_DIM]
            pm_buf[h] = (-kds[:, :, HEAD_DIM:]).astype(BF16)

    def stage_c(c):
        for h in range(DN_HEADS):
            dl = DECAY_LANE + h
            s = s_ref[h]
            sb = s.astype(BF16)
            state_buf[h, c] = sb
            gl = glast_buf[c, SUBLANES - 1:SUBLANES, dl:dl + 1]
            s_ref[h] = s * gl + (jnp.dot(pm_buf[h, c], sb, preferred_element_type=F32) + qm_buf[h, c])

    def stage_d():
        qkd = qkd_buf[...]
        for h in range(DN_HEADS):
            hl = slice(h * HEAD_DIM, (h + 1) * HEAD_DIM)
            wq = jnp.concatenate([w_buf[h], qd_buf[h]], axis=1)
            ws = _bmm(wq, state_buf[h])
            v_new = u_buf[h] - ws[:, :CHUNK]
            o = ws[:, CHUNK:] + _bmm(qkd[:, :, h * CHUNK:(h + 1) * CHUNK], v_new)
            o = o.reshape(tt, HEAD_DIM)
            o = o * _rms_scale(o, HEAD_DIM) * z_buf[:, hl]
            o_buf[:, hl] = o.astype(BF16)
        out_ref[...] = x_ref[...] + jnp.dot(o_buf[...], wout_ref[...], preferred_element_type=F32)

    dense_in()
    for c in range(nc):
        stage_a(c)
    stage_b()
    for c in range(nc):
        stage_c(c)
    stage_d()

    qkv_buf[0:pad, :] = qkv_buf[tt:tt + pad, :]
    cv_buf[0:pad, :] = cv_buf[tt:tt + pad, :]


def _ffn_kernel(x_ref, n2g_ref, wg_ref, wu_ref, wd_ref, fg_ref, out_ref, *, final_norm):
    x = x_ref[...]
    d_model = x.shape[-1]
    hb = (x * _rms_scale(x, d_model) * n2g_ref[...]).astype(BF16)
    gate = jnp.dot(hb, wg_ref[...], preferred_element_type=F32)
    up = jnp.dot(hb, wu_ref[...], preferred_element_type=F32)
    act = (_silu(gate) * up).astype(BF16)
    y = x + jnp.dot(act, wd_ref[...], preferred_element_type=F32)
    if final_norm:
        y = y * _rms_scale(y, d_model) * fg_ref[...]
    out_ref[...] = y


def _resident(shape, index_map):
    return pl.BlockSpec(shape, index_map, pipeline_mode=pl.Buffered(1))


def _mixer_call(x, n1g, win, cw, gp, dng, scw, scg, wout, layer):
    bsz, seq, d_model = x.shape
    tt = min(MIXER_TIME_TILE, seq)
    assert seq % tt == 0 and tt % CHUNK == 0
    sc_width = scg.shape[-1]
    in_cols = win.shape[-1]
    pad = SUBLANES
    nc = tt // CHUNK
    lay = lambda *rest: (lambda b, t: (layer, *rest))
    return pl.pallas_call(
        functools.partial(_mixer_kernel, tt=tt),
        grid=(bsz, seq // tt),
        in_specs=[
            pl.BlockSpec((None, tt, d_model), lambda b, t: (b, t, 0)),
            _resident((None, 1, d_model), lay(0, 0)),
            _resident((None, d_model, in_cols), lay(0, 0)),
            _resident((None, DN_CONV, QKV_COLS), lay(0, 0)),
            _resident((None, SUBLANES, LANES), lay(0, 0)),
            _resident((None, 1, HEAD_DIM), lay(0, 0)),
            _resident((None, SC_CONV, sc_width), lay(0, 0)),
            _resident((None, 1, sc_width), lay(0, 0)),
            _resident((None, d_model, d_model), lay(0, 0)),
        ],
        out_specs=pl.BlockSpec((None, tt, d_model), lambda b, t: (b, t, 0)),
        out_shape=jax.ShapeDtypeStruct(x.shape, F32),
        scratch_shapes=[
            pltpu.VMEM((tt + pad, QKV_COLS), F32),
            pltpu.VMEM((tt, DN_WIDTH), F32),
            pltpu.VMEM((tt, 3 * sc_width), F32),
            pltpu.VMEM((tt, LANES), F32),
            pltpu.VMEM((tt + pad, sc_width), F32),
            pltpu.VMEM((tt, d_model), BF16),
            pltpu.VMEM((DN_HEADS, HEAD_DIM, HEAD_DIM), F32),
            pltpu.VMEM((nc, HEADS_CHUNK, DN_WIDTH), BF16),
            pltpu.VMEM((nc, 2 * CHUNK, DN_WIDTH), BF16),
            pltpu.VMEM((DN_HEADS, nc, CHUNK, 2 * HEAD_DIM), BF16),
            pltpu.VMEM((DN_HEADS, nc, CHUNK, HEAD_DIM), BF16),
            pltpu.VMEM((DN_HEADS, nc, HEAD_DIM, CHUNK), BF16),
            pltpu.VMEM((nc, CHUNK, HEADS_CHUNK), F32),
            pltpu.VMEM((nc, SUBLANES, LANES), F32),
            pltpu.VMEM((DN_HEADS, nc, CHUNK, HEAD_DIM), F32),
            pltpu.VMEM((DN_HEADS, nc, CHUNK, HEAD_DIM), BF16),
            pltpu.VMEM((nc, CHUNK, HEADS_CHUNK), BF16),
            pltpu.VMEM((DN_HEADS, nc, HEAD_DIM, HEAD_DIM), BF16),
            pltpu.VMEM((DN_HEADS, nc, HEAD_DIM, HEAD_DIM), F32),
            pltpu.VMEM((DN_HEADS, nc, HEAD_DIM, HEAD_DIM), BF16),
        ],
        compiler_params=pltpu.CompilerParams(
            dimension_semantics=("arbitrary", "arbitrary"),
            vmem_limit_bytes=MIXER_VMEM_BYTES),
        name=f"mixer_l{layer}",
    )(x, n1g, win, cw, gp, dng, scw, scg, wout)


def _ffn_call(x2d, n2g, wg, wu, wd, fg, layer, final_norm):
    rows, d_model = x2d.shape
    tm = min(FFN_ROW_TILE, rows)
    assert rows % tm == 0
    d_ff = wg.shape[-1]
    lay = lambda *rest: (lambda i: (layer, *rest))
    return pl.pallas_call(
        functools.partial(_ffn_kernel, final_norm=final_norm),
        grid=(rows // tm,),
        in_specs=[
            pl.BlockSpec((tm, d_model), lambda i: (i, 0)),
            _resident((None, 1, d_model), lay(0, 0)),
            _resident((None, d_model, d_ff), lay(0, 0)),
            _resident((None, d_model, d_ff), lay(0, 0)),
            _resident((None, d_ff, d_model), lay(0, 0)),
            _resident((1, d_model), lambda i: (0, 0)),
        ],
        out_specs=pl.BlockSpec((tm, d_model), lambda i: (i, 0)),
        out_shape=jax.ShapeDtypeStruct(x2d.shape, F32),
        compiler_params=pltpu.CompilerParams(
            dimension_semantics=("arbitrary",),
            vmem_limit_bytes=FFN_VMEM_BYTES),
        name=f"ffn_l{layer}",
    )(x2d, n2g, wg, wu, wd, fg)


def kernel(x, norm1_g, w_in, dn_conv_w, dn_a_log, dn_dt_bias, dn_norm_g, sc_conv_w, sc_norm_g,
           w_out, norm2_g, ffn_w_gate, ffn_w_up, ffn_w_down, final_norm_g):
    depth, d_model, _ = w_in.shape
    bsz, seq, _ = x.shape
    sc_width = sc_norm_g.shape[-1]

    o1 = QKV_COLS + DN_WIDTH
    o2 = o1 + 2 * DN_HEADS
    small = jnp.pad(w_in[..., o1:o2], ((0, 0), (0, 0), (0, LANES - 2 * DN_HEADS)))
    win = jnp.concatenate([w_in[..., :o1], w_in[..., o2:], small], axis=-1).astype(BF16)
    wout = w_out.astype(BF16)
    wg = ffn_w_gate.astype(BF16)
    wu = ffn_w_up.astype(BF16)
    wd = ffn_w_down.astype(BF16)

    gp = jnp.zeros((depth, SUBLANES, LANES), F32)
    gp = gp.at[:, 0, DECAY_LANE:DECAY_LANE + DN_HEADS].set(dn_a_log.astype(F32))
    gp = gp.at[:, 1, DECAY_LANE:DECAY_LANE + DN_HEADS].set(dn_dt_bias.astype(F32))

    n1g = norm1_g.reshape(depth, 1, d_model)
    n2g = norm2_g.reshape(depth, 1, d_model)
    dng = dn_norm_g.reshape(depth, 1, HEAD_DIM)
    scg = sc_norm_g.reshape(depth, 1, sc_width)
    fg = final_norm_g.reshape(1, d_model)

    for layer in range(depth):
        x = _mixer_call(x, n1g, win, dn_conv_w, gp, dng, sc_conv_w, scg, wout, layer)
        x = _ffn_call(x.reshape(bsz * seq, d_model), n2g, wg, wu, wd, fg, layer,
                      final_norm=(layer == depth - 1)).reshape(bsz, seq, d_model)
    return x
```

```python
import functools

import jax
import jax.numpy as jnp
from jax import lax
from jax.experimental import pallas as pl
from jax.experimental.pallas import tpu as pltpu

F32 = jnp.float32
BF16 = jnp.bfloat16

EPS = 1e-6
LANES = 128
SUBLANES = 8
DN_HEADS = 4
HEAD_DIM = 128
DN_WIDTH = DN_HEADS * HEAD_DIM
DN_CONV = 4
SC_GROUPS = 4
SC_CONV = 3
CHUNK = 64
HEADS_CHUNK = DN_HEADS * CHUNK
QKV_COLS = 3 * DN_WIDTH
Z_OFF = QKV_COLS
SC_OFF = Z_OFF + DN_WIDTH
BETA_LANE = 0
DECAY_LANE = DN_HEADS

MIXER_TIME_TILE = 512
FFN_ROW_TILE = 512
MIXER_VMEM_BYTES = 52 * 1024 * 1024
FFN_VMEM_BYTES = 56 * 1024 * 1024


def _sigmoid(v):
    return 1.0 / (1.0 + jnp.exp(-v))


def _silu(v):
    half = 0.5 * v
    return half + half * jnp.tanh(half)


def _softplus(v):
    return jnp.maximum(v, 0.0) + jnp.log(1.0 + jnp.exp(-jnp.abs(v)))


def _split3(v):
    hi = v.astype(BF16)
    r1 = v - hi.astype(F32)
    mid = r1.astype(BF16)
    lo = (r1 - mid.astype(F32)).astype(BF16)
    return hi, mid, lo


def _bmm(a, b):
    return lax.dot_general(a.astype(BF16), b.astype(BF16), (((2,), (1,)), ((0,), (0,))),
                           preferred_element_type=F32)


def _bmm_nt(a, b):
    return lax.dot_general(a.astype(BF16), b.astype(BF16), (((2,), (2,)), ((0,), (0,))),
                           preferred_element_type=F32)


def _head_block_diag(xb, lane_head):
    zero = jnp.zeros_like(xb)
    return jnp.concatenate([jnp.where(lane_head == r, xb, zero) for r in range(DN_HEADS)], axis=1)


def _unit_lower_inverse_heads(lower, row_all, lane_in, lane_head):
    c = lower.shape[1]

    def off_diag(s):
        same_block = (row_all & -(2 * s)) == (lane_in & -(2 * s))
        return same_block & ((row_all & s) != 0) & ((lane_in & s) == 0)

    t = (row_all == lane_in).astype(F32) - jnp.where(off_diag(1), lower, 0.0)
    s = 2
    while s < c:
        x = _bmm(jnp.where(off_diag(s), lower, 0.0), _head_block_diag(t.astype(BF16), lane_head))
        t = t - _bmm(t, _head_block_diag(x.astype(BF16), lane_head))
        s *= 2
    return t


def _rms_scale(v, width):
    return lax.rsqrt(jnp.sum(v * v, axis=-1, keepdims=True) * (1.0 / width) + EPS)


def _mixer_kernel(x_ref, n1g_ref, win_ref, cw_ref, gp_ref, dng_ref, scw_ref, scg_ref, wout_ref,
                  out_ref,
                  qkv_buf, z_buf, scin_buf, small_buf, cv_buf, o_buf, s_ref,
                  bdk_buf, kbq_buf, rhs_buf, qd_buf, kdt_buf, decay_buf, glast_buf,
                  u_buf, w_buf, qkd_buf, pm_buf, qm_buf, state_buf, *, tt):
    t = pl.program_id(1)
    sc_width = scg_ref.shape[-1]
    pad = SUBLANES
    nc = tt // CHUNK
    d_model = x_ref.shape[-1]

    @pl.when(jnp.logical_and(pl.program_id(0) == 0, t == 0))
    def _():
        bdk_buf[...] = jnp.zeros_like(bdk_buf)

    @pl.when(t == 0)
    def _():
        s_ref[...] = jnp.zeros_like(s_ref)
        qkv_buf[0:pad, :] = jnp.zeros((pad, QKV_COLS), F32)
        cv_buf[0:pad, :] = jnp.zeros((pad, sc_width), F32)

    row = lax.broadcasted_iota(jnp.int32, (CHUNK, CHUNK), 0)
    col = lax.broadcasted_iota(jnp.int32, (CHUNK, CHUNK), 1)
    causal = row >= col
    tri = causal.astype(BF16)
    row_all = lax.broadcasted_iota(jnp.int32, (CHUNK, HEADS_CHUNK), 0)
    lane_all = lax.broadcasted_iota(jnp.int32, (CHUNK, HEADS_CHUNK), 1)
    lane_head = lane_all // CHUNK
    lane_in = lane_all - lane_head * CHUNK
    strict_all = row_all > lane_in
    q_scale = HEAD_DIM ** -0.5
    neg_rate = -jnp.exp(gp_ref[0:1, :])
    dt_bias = gp_ref[1:2, :]
    small_off = SC_OFF + 3 * sc_width
    gw = sc_width // SC_GROUPS

    def dense_in():
        x = x_ref[...]
        hb = (x * _rms_scale(x, d_model) * n1g_ref[...]).astype(BF16)
        dot = lambda lo, hi: jnp.dot(hb, win_ref[:, lo:hi], preferred_element_type=F32)
        qkv_buf[pad:pad + tt, :] = dot(0, QKV_COLS)
        z_buf[...] = dot(Z_OFF, Z_OFF + DN_WIDTH)
        scin_buf[...] = dot(SC_OFF, small_off)
        small_buf[...] = dot(small_off, small_off + LANES)

    def stage_a(c):
        r0 = c * CHUNK
        rows = slice(r0, r0 + CHUNK)

        sm = small_buf[rows, :]
        beta_blk = _sigmoid(sm)
        g_raw = neg_rate * _softplus(sm + dt_bias)
        g_hi, g_mid, g_lo = _split3(g_raw)
        cum = lambda part: jnp.dot(tri, part, preferred_element_type=F32)
        g_blk = cum(g_hi) + (cum(g_mid) + cum(g_lo))
        g_t = g_blk.T
        eg_blk = jnp.exp(g_blk)
        g_last_blk = jnp.broadcast_to(g_blk[CHUNK - 1:CHUNK, :], (CHUNK, LANES))
        kdec_blk = jnp.exp(g_last_blk - g_blk)
        glast_buf[c] = eg_blk[CHUNK - SUBLANES:CHUNK, :]

        decays = []
        for h in range(DN_HEADS):
            def conv_silu(col0):
                assert DN_CONV == 4
                cols = slice(col0, col0 + HEAD_DIM)
                win = qkv_buf[r0:r0 + pad + CHUNK, cols]
                prev = jnp.concatenate([win[:1], win[:-1]], axis=0)
                w = [cw_ref[j:j + 1, cols] for j in range(DN_CONV)]
                near = w[3] * win + w[2] * prev
                far = w[1] * win + w[0] * prev
                return _silu(near[pad:pad + CHUNK] + far[pad - 2:pad - 2 + CHUNK])

            qh = conv_silu(h * HEAD_DIM)
            kh = conv_silu(DN_WIDTH + h * HEAD_DIM)
            vh = conv_silu(2 * DN_WIDTH + h * HEAD_DIM)
            qn = qh * (lax.rsqrt(jnp.sum(qh * qh, axis=-1, keepdims=True) + EPS) * q_scale)
            kn = kh * lax.rsqrt(jnp.sum(kh * kh, axis=-1, keepdims=True) + EPS)

            bl = BETA_LANE + h
            dl = DECAY_LANE + h
            beta_c = beta_blk[:, bl:bl + 1]
            g_c = g_blk[:, dl:dl + 1]
            eg_c = eg_blk[:, dl:dl + 1]
            kdec_c = kdec_blk[:, dl:dl + 1]
            g_r = g_t[dl:dl + 1, :]
            decays.append(jnp.exp(jnp.where(causal, g_c - g_r, -jnp.inf)))

            kb = kn * beta_c
            hl = slice(h * HEAD_DIM, (h + 1) * HEAD_DIM)
            bdk_buf[c, h * CHUNK:(h + 1) * CHUNK, hl] = kn.astype(BF16)
            kbq_buf[c, 0:CHUNK, hl] = kb.astype(BF16)
            kbq_buf[c, CHUNK:2 * CHUNK, hl] = qn.astype(BF16)
            rhs_buf[h, c, :, 0:HEAD_DIM] = (vh * beta_c).astype(BF16)
            rhs_buf[h, c, :, HEAD_DIM:2 * HEAD_DIM] = (kb * eg_c).astype(BF16)
            qd_buf[h, c] = (qn * eg_c).astype(BF16)
            kdt_buf[h, c] = (kn * kdec_c).T.astype(BF16)

            zh = z_buf[rows, hl]
            z_buf[rows, hl] = dng_ref[...] * _silu(zh)
        decay_buf[c] = jnp.concatenate(decays, axis=1)

        cv = scin_buf[rows, sc_width:2 * sc_width] * scin_buf[rows, 2 * sc_width:3 * sc_width]
        cv_buf[pad + r0:pad + r0 + CHUNK, :] = cv
        first = r0 + pad - (SC_CONV - 1)
        acc = scw_ref[SC_CONV - 1:SC_CONV, :] * cv
        for j in range(SC_CONV - 1):
            acc = acc + scw_ref[j:j + 1, :] * cv_buf[first + j:first + j + CHUNK, :]
        y = scin_buf[rows, 0:sc_width] * acc
        for g in range(SC_GROUPS):
            yg = y[:, g * gw:(g + 1) * gw]
            yg = yg * _rms_scale(yg, gw) * scg_ref[:, g * gw:(g + 1) * gw]
            o_buf[rows, DN_WIDTH + g * gw:DN_WIDTH + (g + 1) * gw] = yg.astype(BF16)

    def stage_b():
        kq = _bmm_nt(kbq_buf[...], bdk_buf[...])
        decay = decay_buf[...]
        lower = jnp.where(strict_all, kq[:, :CHUNK] * decay, 0.0)
        qkd_buf[...] = (kq[:, CHUNK:] * decay).astype(BF16)
        t_all = _unit_lower_inverse_heads(lower, row_all, lane_in, lane_head).astype(BF16)
        for h in range(DN_HEADS):
            sol = _bmm(t_all[:, :, h * CHUNK:(h + 1) * CHUNK], rhs_buf[h])
            u_buf[h] = sol[:, :, :HEAD_DIM]
            w_buf[h] = sol[:, :, HEAD_DIM:].astype(BF16)
            kds = _bmm(kdt_buf[h], sol)
            qm_buf[h] = kds[:, :, :HEAD_DIM]
            pm_buf[h] = (-kds[:, :, HEAD_DIM:]).astype(BF16)

    def stage_c(c):
        for h in range(DN_HEADS):
            dl = DECAY_LANE + h
            s = s_ref[h]
            sb = s.astype(BF16)
            state_buf[h, c] = sb
            gl = glast_buf[c, SUBLANES - 1:SUBLANES, dl:dl + 1]
            s_ref[h] = s * gl + (jnp.dot(pm_buf[h, c], sb, preferred_element_type=F32) + qm_buf[h, c])

    def stage_d():
        qkd = qkd_buf[...]
        for h in range(DN_HEADS):
            hl = slice(h * HEAD_DIM, (h + 1) * HEAD_DIM)
            wq = jnp.concatenate([w_buf[h], qd_buf[h]], axis=1)
            ws = _bmm(wq, state_buf[h])
            v_new = u_buf[h] - ws[:, :CHUNK]
            o = ws[:, CHUNK:] + _bmm(qkd[:, :, h * CHUNK:(h + 1) * CHUNK], v_new)
            o = o.reshape(tt, HEAD_DIM)
            o = o * _rms_scale(o, HEAD_DIM) * z_buf[:, hl]
            o_buf[:, hl] = o.astype(BF16)
        out_ref[...] = x_ref[...] + jnp.dot(o_buf[...], wout_ref[...], preferred_element_type=F32)

    dense_in()
    for c in range(nc):
        stage_a(c)
    stage_b()
    for c in range(nc):
        stage_c(c)
    stage_d()

    qkv_buf[0:pad, :] = qkv_buf[tt:tt + pad, :]
    cv_buf[0:pad, :] = cv_buf[tt:tt + pad, :]


def _ffn_kernel(x_ref, n2g_ref, wg_ref, wu_ref, wd_ref, fg_ref, out_ref, *, final_norm):
    x = x_ref[...]
    d_model = x.shape[-1]
    hb = (x * _rms_scale(x, d_model) * n2g_ref[...]).astype(BF16)
    gate = jnp.dot(hb, wg_ref[...], preferred_element_type=F32)
    up = jnp.dot(hb, wu_ref[...], preferred_element_type=F32)
    act = (_silu(gate) * up).astype(BF16)
    y = x + jnp.dot(act, wd_ref[...], preferred_element_type=F32)
    if final_norm:
        y = y * _rms_scale(y, d_model) * fg_ref[...]
    out_ref[...] = y


def _resident(shape, index_map):
    return pl.BlockSpec(shape, index_map, pipeline_mode=pl.Buffered(1))


def _mixer_call(x, n1g, win, cw, gp, dng, scw, scg, wout, layer):
    bsz, seq, d_model = x.shape
    tt = min(MIXER_TIME_TILE, seq)
    assert seq % tt == 0 and tt % CHUNK == 0
    sc_width = scg.shape[-1]
    in_cols = win.shape[-1]
    pad = SUBLANES
    nc = tt // CHUNK
    lay = lambda *rest: (lambda b, t: (layer, *rest))
    return pl.pallas_call(
        functools.partial(_mixer_kernel, tt=tt),
        grid=(bsz, seq // tt),
        in_specs=[
            pl.BlockSpec((None, tt, d_model), lambda b, t: (b, t, 0)),
            _resident((None, 1, d_model), lay(0, 0)),
            _resident((None, d_model, in_cols), lay(0, 0)),
            _resident((None, DN_CONV, QKV_COLS), lay(0, 0)),
            _resident((None, SUBLANES, LANES), lay(0, 0)),
            _resident((None, 1, HEAD_DIM), lay(0, 0)),
            _resident((None, SC_CONV, sc_width), lay(0, 0)),
            _resident((None, 1, sc_width), lay(0, 0)),
            _resident((None, d_model, d_model), lay(0, 0)),
        ],
        out_specs=pl.BlockSpec((None, tt, d_model), lambda b, t: (b, t, 0)),
        out_shape=jax.ShapeDtypeStruct(x.shape, F32),
        scratch_shapes=[
            pltpu.VMEM((tt + pad, QKV_COLS), F32),
            pltpu.VMEM((tt, DN_WIDTH), F32),
            pltpu.VMEM((tt, 3 * sc_width), F32),
            pltpu.VMEM((tt, LANES), F32),
            pltpu.VMEM((tt + pad, sc_width), F32),
            pltpu.VMEM((tt, d_model), BF16),
            pltpu.VMEM((DN_HEADS, HEAD_DIM, HEAD_DIM), F32),
            pltpu.VMEM((nc, HEADS_CHUNK, DN_WIDTH), BF16),
            pltpu.VMEM((nc, 2 * CHUNK, DN_WIDTH), BF16),
            pltpu.VMEM((DN_HEADS, nc, CHUNK, 2 * HEAD_DIM), BF16),
            pltpu.VMEM((DN_HEADS, nc, CHUNK, HEAD_DIM), BF16),
            pltpu.VMEM((DN_HEADS, nc, HEAD_DIM, CHUNK), BF16),
            pltpu.VMEM((nc, CHUNK, HEADS_CHUNK), F32),
            pltpu.VMEM((nc, SUBLANES, LANES), F32),
            pltpu.VMEM((DN_HEADS, nc, CHUNK, HEAD_DIM), F32),
            pltpu.VMEM((DN_HEADS, nc, CHUNK, HEAD_DIM), BF16),
            pltpu.VMEM((nc, CHUNK, HEADS_CHUNK), BF16),
            pltpu.VMEM((DN_HEADS, nc, HEAD_DIM, HEAD_DIM), BF16),
            pltpu.VMEM((DN_HEADS, nc, HEAD_DIM, HEAD_DIM), F32),
            pltpu.VMEM((DN_HEADS, nc, HEAD_DIM, HEAD_DIM), BF16),
        ],
        compiler_params=pltpu.CompilerParams(
            dimension_semantics=("arbitrary", "arbitrary"),
            vmem_limit_bytes=MIXER_VMEM_BYTES),
        name=f"mixer_l{layer}",
    )(x, n1g, win, cw, gp, dng, scw, scg, wout)


def _ffn_call(x2d, n2g, wg, wu, wd, fg, layer, final_norm):
    rows, d_model = x2d.shape
    tm = min(FFN_ROW_TILE, rows)
    assert rows % tm == 0
    d_ff = wg.shape[-1]
    lay = lambda *rest: (lambda i: (layer, *rest))
    return pl.pallas_call(
        functools.partial(_ffn_kernel, final_norm=final_norm),
        grid=(rows // tm,),
        in_specs=[
            pl.BlockSpec((tm, d_model), lambda i: (i, 0)),
            _resident((None, 1, d_model), lay(0, 0)),
            _resident((None, d_model, d_ff), lay(0, 0)),
            _resident((None, d_model, d_ff), lay(0, 0)),
            _resident((None, d_ff, d_model), lay(0, 0)),
            _resident((1, d_model), lambda i: (0, 0)),
        ],
        out_specs=pl.BlockSpec((tm, d_model), lambda i: (i, 0)),
        out_shape=jax.ShapeDtypeStruct(x2d.shape, F32),
        compiler_params=pltpu.CompilerParams(
            dimension_semantics=("arbitrary",),
            vmem_limit_bytes=FFN_VMEM_BYTES),
        name=f"ffn_l{layer}",
    )(x2d, n2g, wg, wu, wd, fg)


def kernel(x, norm1_g, w_in, dn_conv_w, dn_a_log, dn_dt_bias, dn_norm_g, sc_conv_w, sc_norm_g,
           w_out, norm2_g, ffn_w_gate, ffn_w_up, ffn_w_down, final_norm_g):
    depth, d_model, _ = w_in.shape
    bsz, seq, _ = x.shape
    sc_width = sc_norm_g.shape[-1]

    o1 = QKV_COLS + DN_WIDTH
    o2 = o1 + 2 * DN_HEADS
    small = jnp.pad(w_in[..., o1:o2], ((0, 0), (0, 0), (0, LANES - 2 * DN_HEADS)))
    win = jnp.concatenate([w_in[..., :o1], w_in[..., o2:], small], axis=-1).astype(BF16)
    wout = w_out.astype(BF16)
    wg = ffn_w_gate.astype(BF16)
    wu = ffn_w_up.astype(BF16)
    wd = ffn_w_down.astype(BF16)

    gp = jnp.zeros((depth, SUBLANES, LANES), F32)
    gp = gp.at[:, 0, DECAY_LANE:DECAY_LANE + DN_HEADS].set(dn_a_log.astype(F32))
    gp = gp.at[:, 1, DECAY_LANE:DECAY_LANE + DN_HEADS].set(dn_dt_bias.astype(F32))

    n1g = norm1_g.reshape(depth, 1, d_model)
    n2g = norm2_g.reshape(depth, 1, d_model)
    dng = dn_norm_g.reshape(depth, 1, HEAD_DIM)
    scg = sc_norm_g.reshape(depth, 1, sc_width)
    fg = final_norm_g.reshape(1, d_model)

    for layer in range(depth):
        x = _mixer_call(x, n1g, win, dn_conv_w, gp, dng, sc_conv_w, scg, wout, layer)
        x = _ffn_call(x.reshape(bsz * seq, d_model), n2g, wg, wu, wd, fg, layer,
                      final_norm=(layer == depth - 1)).reshape(bsz, seq, d_model)
    return x
```

```python
import functools

import jax
import jax.numpy as jnp
from jax import lax
from jax.experimental import pallas as pl
from jax.experimental.pallas import tpu as pltpu

F32 = jnp.float32
BF16 = jnp.bfloat16

EPS = 1e-6
LANES = 128
SUBLANES = 8
DN_HEADS = 4
HEAD_DIM = 128
DN_WIDTH = DN_HEADS * HEAD_DIM
DN_CONV = 4
SC_GROUPS = 4
SC_CONV = 3
CHUNK = 64
HEADS_CHUNK = DN_HEADS * CHUNK
QKV_COLS = 3 * DN_WIDTH
QKV_SLABS = QKV_COLS // LANES
Z_OFF = QKV_COLS
SC_OFF = Z_OFF + DN_WIDTH
BETA_LANE = 0
DECAY_LANE = DN_HEADS

MIXER_TIME_TILE = 512
FFN_ROW_TILE = 512
MIXER_VMEM_BYTES = 52 * 1024 * 1024
FFN_VMEM_BYTES = 56 * 1024 * 1024


def _sigmoid(v):
    return 1.0 / (1.0 + jnp.exp(-v))


def _silu(v):
    half = 0.5 * v
    return half + half * jnp.tanh(half)


def _softplus(v):
    return jnp.maximum(v, 0.0) + jnp.log(1.0 + jnp.exp(-jnp.abs(v)))


def _split3(v):
    hi = v.astype(BF16)
    r1 = v - hi.astype(F32)
    mid = r1.astype(BF16)
    lo = (r1 - mid.astype(F32)).astype(BF16)
    return hi, mid, lo


def _bmm(a, b):
    return lax.dot_general(a.astype(BF16), b.astype(BF16), (((2,), (1,)), ((0,), (0,))),
                           preferred_element_type=F32)


def _bmm_nt(a, b):
    return lax.dot_general(a.astype(BF16), b.astype(BF16), (((2,), (2,)), ((0,), (0,))),
                           preferred_element_type=F32)


def _head_block_diag(xb, lane_head):
    zero = jnp.zeros_like(xb)
    return jnp.concatenate([jnp.where(lane_head == r, xb, zero) for r in range(DN_HEADS)], axis=1)


def _unit_lower_inverse_heads(lower, row_all, lane_in, lane_head):
    c = lower.shape[1]

    def off_diag(s):
        same_block = (row_all & -(2 * s)) == (lane_in & -(2 * s))
        return same_block & ((row_all & s) != 0) & ((lane_in & s) == 0)

    t = (row_all == lane_in).astype(F32) - jnp.where(off_diag(1), lower, 0.0)
    s = 2
    while s < c:
        x = _bmm(jnp.where(off_diag(s), lower, 0.0), _head_block_diag(t.astype(BF16), lane_head))
        t = t - _bmm(t, _head_block_diag(x.astype(BF16), lane_head))
        s *= 2
    return t


def _rms_scale(v, width):
    return lax.rsqrt(jnp.sum(v * v, axis=-1, keepdims=True) * (1.0 / width) + EPS)


def _mixer_kernel(x_ref, n1g_ref, win_ref, cw_ref, gp_ref, dng_ref, scw_ref, scg_ref, wout_ref,
                  out_ref,
                  qkv_buf, z_buf, scin_buf, small_buf, cv_buf, o_buf, s_ref,
                  bdk_buf, kbq_buf, rhs_buf, qd_buf, kdt_buf, decay_buf, glast_buf,
                  u_buf, w_buf, qkd_buf, pm_buf, qm_buf, state_buf, *, tt):
    t = pl.program_id(1)
    sc_width = scg_ref.shape[-1]
    pad = SUBLANES
    nc = tt // CHUNK
    d_model = x_ref.shape[-1]

    @pl.when(jnp.logical_and(pl.program_id(0) == 0, t == 0))
    def _():
        bdk_buf[...] = jnp.zeros_like(bdk_buf)

    @pl.when(t == 0)
    def _():
        s_ref[...] = jnp.zeros_like(s_ref)
        qkv_buf[:, 0:pad, :] = jnp.zeros((QKV_SLABS, pad, LANES), F32)
        cv_buf[0:pad, :] = jnp.zeros((pad, sc_width), F32)

    row = lax.broadcasted_iota(jnp.int32, (CHUNK, CHUNK), 0)
    col = lax.broadcasted_iota(jnp.int32, (CHUNK, CHUNK), 1)
    causal = row >= col
    tri = causal.astype(BF16)
    row_all = lax.broadcasted_iota(jnp.int32, (CHUNK, HEADS_CHUNK), 0)
    lane_all = lax.broadcasted_iota(jnp.int32, (CHUNK, HEADS_CHUNK), 1)
    lane_head = lane_all // CHUNK
    lane_in = lane_all - lane_head * CHUNK
    strict_all = row_all > lane_in
    q_scale = HEAD_DIM ** -0.5
    neg_rate = -jnp.exp(gp_ref[0:1, :])
    dt_bias = gp_ref[1:2, :]
    small_off = SC_OFF + 3 * sc_width
    gw = sc_width // SC_GROUPS

    def dense_in():
        x = x_ref[...]
        hb = (x * _rms_scale(x, d_model) * n1g_ref[...]).astype(BF16)
        dot = lambda lo, hi: jnp.dot(hb, win_ref[:, lo:hi], preferred_element_type=F32)
        qkv = dot(0, QKV_COLS)
        for j in range(QKV_SLABS):
            qkv_buf[j, pad:pad + tt, :] = qkv[:, j * LANES:(j + 1) * LANES]
        z_buf[...] = dot(Z_OFF, Z_OFF + DN_WIDTH)
        scin_buf[...] = dot(SC_OFF, small_off)
        small_buf[...] = dot(small_off, small_off + LANES)

    def stage_a(c):
        r0 = c * CHUNK
        rows = slice(r0, r0 + CHUNK)

        sm = small_buf[rows, :]
        beta_blk = _sigmoid(sm)
        g_raw = neg_rate * _softplus(sm + dt_bias)
        g_hi, g_mid, g_lo = _split3(g_raw)
        cum = lambda part: jnp.dot(tri, part, preferred_element_type=F32)
        g_blk = cum(g_hi) + (cum(g_mid) + cum(g_lo))
        g_t = g_blk.T
        eg_blk = jnp.exp(g_blk)
        g_last_blk = jnp.broadcast_to(g_blk[CHUNK - 1:CHUNK, :], (CHUNK, LANES))
        kdec_blk = jnp.exp(g_last_blk - g_blk)
        glast_buf[c] = eg_blk[CHUNK - SUBLANES:CHUNK, :]

        decays = []
        for h in range(DN_HEADS):
            def conv_silu(col0):
                slab = col0 // LANES
                cols = slice(col0, col0 + HEAD_DIM)
                first = r0 + pad - (DN_CONV - 1)
                tap = lambda j: cw_ref[j:j + 1, cols] * qkv_buf[slab, pl.ds(first + j, CHUNK, stride=1), :]
                acc = tap(DN_CONV - 1)
                for j in range(DN_CONV - 1):
                    acc = acc + tap(j)
                return _silu(acc)

            qh = conv_silu(h * HEAD_DIM)
            kh = conv_silu(DN_WIDTH + h * HEAD_DIM)
            vh = conv_silu(2 * DN_WIDTH + h * HEAD_DIM)
            qn = qh * (lax.rsqrt(jnp.sum(qh * qh, axis=-1, keepdims=True) + EPS) * q_scale)
            kn = kh * lax.rsqrt(jnp.sum(kh * kh, axis=-1, keepdims=True) + EPS)

            bl = BETA_LANE + h
            dl = DECAY_LANE + h
            beta_c = beta_blk[:, bl:bl + 1]
            g_c = g_blk[:, dl:dl + 1]
            eg_c = eg_blk[:, dl:dl + 1]
            kdec_c = kdec_blk[:, dl:dl + 1]
            g_r = g_t[dl:dl + 1, :]
            decays.append(jnp.exp(jnp.where(causal, g_c - g_r, -jnp.inf)))

            kb = kn * beta_c
            hl = slice(h * HEAD_DIM, (h + 1) * HEAD_DIM)
            bdk_buf[c, h * CHUNK:(h + 1) * CHUNK, hl] = kn.astype(BF16)
            kbq_buf[c, 0:CHUNK, hl] = kb.astype(BF16)
            kbq_buf[c, CHUNK:2 * CHUNK, hl] = qn.astype(BF16)
            rhs_buf[h, c, :, 0:HEAD_DIM] = (vh * beta_c).astype(BF16)
            rhs_buf[h, c, :, HEAD_DIM:2 * HEAD_DIM] = (kb * eg_c).astype(BF16)
            qd_buf[h, c] = (qn * eg_c).astype(BF16)
            kdt_buf[h, c] = (kn * kdec_c).T.astype(BF16)

            zh = z_buf[rows, hl]
            z_buf[rows, hl] = dng_ref[...] * _silu(zh)
        decay_buf[c] = jnp.concatenate(decays, axis=1)

        cv = scin_buf[rows, sc_width:2 * sc_width] * scin_buf[rows, 2 * sc_width:3 * sc_width]
        cv_buf[pad + r0:pad + r0 + CHUNK, :] = cv
        first = r0 + pad - (SC_CONV - 1)
        acc = scw_ref[SC_CONV - 1:SC_CONV, :] * cv
        for j in range(SC_CONV - 1):
            acc = acc + scw_ref[j:j + 1, :] * cv_buf[first + j:first + j + CHUNK, :]
        y = scin_buf[rows, 0:sc_width] * acc
        for g in range(SC_GROUPS):
            yg = y[:, g * gw:(g + 1) * gw]
            yg = yg * _rms_scale(yg, gw) * scg_ref[:, g * gw:(g + 1) * gw]
            o_buf[rows, DN_WIDTH + g * gw:DN_WIDTH + (g + 1) * gw] = yg.astype(BF16)

    def stage_b():
        kq = _bmm_nt(kbq_buf[...], bdk_buf[...])
        decay = decay_buf[...]
        lower = jnp.where(strict_all, kq[:, :CHUNK] * decay, 0.0)
        qkd_buf[...] = (kq[:, CHUNK:] * decay).astype(BF16)
        t_all = _unit_lower_inverse_heads(lower, row_all, lane_in, lane_head).astype(BF16)
        for h in range(DN_HEADS):
            sol = _bmm(t_all[:, :, h * CHUNK:(h + 1) * CHUNK], rhs_buf[h])
            u_buf[h] = sol[:, :, :HEAD_DIM]
            w_buf[h] = sol[:, :, HEAD_DIM:].astype(BF16)
            kds = _bmm(kdt_buf[h], sol)
            qm_buf[h] = kds[:, :, :HEAD_DIM]
            pm_buf[h] = (-kds[:, :, HEAD_DIM:]).astype(BF16)

    def stage_c(c):
        for h in range(DN_HEADS):
            dl = DECAY_LANE + h
            s = s_ref[h]
            sb = s.astype(BF16)
            state_buf[h, c] = sb
            gl = glast_buf[c, SUBLANES - 1:SUBLANES, dl:dl + 1]
            s_ref[h] = s * gl + (jnp.dot(pm_buf[h, c], sb, preferred_element_type=F32) + qm_buf[h, c])

    def stage_d():
        qkd = qkd_buf[...]
        for h in range(DN_HEADS):
            hl = slice(h * HEAD_DIM, (h + 1) * HEAD_DIM)
            wq = jnp.concatenate([w_buf[h], qd_buf[h]], axis=1)
            ws = _bmm(wq, state_buf[h])
            v_new = u_buf[h] - ws[:, :CHUNK]
            o = ws[:, CHUNK:] + _bmm(qkd[:, :, h * CHUNK:(h + 1) * CHUNK], v_new)
            o = o.reshape(tt, HEAD_DIM)
            o = o * _rms_scale(o, HEAD_DIM) * z_buf[:, hl]
            o_buf[:, hl] = o.astype(BF16)
        out_ref[...] = x_ref[...] + jnp.dot(o_buf[...], wout_ref[...], preferred_element_type=F32)

    dense_in()
    for c in range(nc):
        stage_a(c)
    stage_b()
    for c in range(nc):
        stage_c(c)
    stage_d()

    qkv_buf[:, 0:pad, :] = qkv_buf[:, tt:tt + pad, :]
    cv_buf[0:pad, :] = cv_buf[tt:tt + pad, :]


def _ffn_kernel(x_ref, n2g_ref, wg_ref, wu_ref, wd_ref, fg_ref, out_ref, *, final_norm):
    x = x_ref[...]
    d_model = x.shape[-1]
    hb = (x * _rms_scale(x, d_model) * n2g_ref[...]).astype(BF16)
    gate = jnp.dot(hb, wg_ref[...], preferred_element_type=F32)
    up = jnp.dot(hb, wu_ref[...], preferred_element_type=F32)
    act = (_silu(gate) * up).astype(BF16)
    y = x + jnp.dot(act, wd_ref[...], preferred_element_type=F32)
    if final_norm:
        y = y * _rms_scale(y, d_model) * fg_ref[...]
    out_ref[...] = y


def _resident(shape, index_map):
    return pl.BlockSpec(shape, index_map, pipeline_mode=pl.Buffered(1))


def _mixer_call(x, n1g, win, cw, gp, dng, scw, scg, wout, layer):
    bsz, seq, d_model = x.shape
    tt = min(MIXER_TIME_TILE, seq)
    assert seq % tt == 0 and tt % CHUNK == 0
    sc_width = scg.shape[-1]
    in_cols = win.shape[-1]
    pad = SUBLANES
    nc = tt // CHUNK
    lay = lambda *rest: (lambda b, t: (layer, *rest))
    return pl.pallas_call(
        functools.partial(_mixer_kernel, tt=tt),
        grid=(bsz, seq // tt),
        in_specs=[
            pl.BlockSpec((None, tt, d_model), lambda b, t: (b, t, 0)),
            _resident((None, 1, d_model), lay(0, 0)),
            _resident((None, d_model, in_cols), lay(0, 0)),
            _resident((None, DN_CONV, QKV_COLS), lay(0, 0)),
            _resident((None, SUBLANES, LANES), lay(0, 0)),
            _resident((None, 1, HEAD_DIM), lay(0, 0)),
            _resident((None, SC_CONV, sc_width), lay(0, 0)),
            _resident((None, 1, sc_width), lay(0, 0)),
            _resident((None, d_model, d_model), lay(0, 0)),
        ],
        out_specs=pl.BlockSpec((None, tt, d_model), lambda b, t: (b, t, 0)),
        out_shape=jax.ShapeDtypeStruct(x.shape, F32),
        scratch_shapes=[
            pltpu.VMEM((QKV_SLABS, tt + pad, LANES), F32),
            pltpu.VMEM((tt, DN_WIDTH), F32),
            pltpu.VMEM((tt, 3 * sc_width), F32),
            pltpu.VMEM((tt, LANES), F32),
            pltpu.VMEM((tt + pad, sc_width), F32),
            pltpu.VMEM((tt, d_model), BF16),
            pltpu.VMEM((DN_HEADS, HEAD_DIM, HEAD_DIM), F32),
            pltpu.VMEM((nc, HEADS_CHUNK, DN_WIDTH), BF16),
            pltpu.VMEM((nc, 2 * CHUNK, DN_WIDTH), BF16),
            pltpu.VMEM((DN_HEADS, nc, CHUNK, 2 * HEAD_DIM), BF16),
            pltpu.VMEM((DN_HEADS, nc, CHUNK, HEAD_DIM), BF16),
            pltpu.VMEM((DN_HEADS, nc, HEAD_DIM, CHUNK), BF16),
            pltpu.VMEM((nc, CHUNK, HEADS_CHUNK), F32),
            pltpu.VMEM((nc, SUBLANES, LANES), F32),
            pltpu.VMEM((DN_HEADS, nc, CHUNK, HEAD_DIM), F32),
            pltpu.VMEM((DN_HEADS, nc, CHUNK, HEAD_DIM), BF16),
            pltpu.VMEM((nc, CHUNK, HEADS_CHUNK), BF16),
            pltpu.VMEM((DN_HEADS, nc, HEAD_DIM, HEAD_DIM), BF16),
            pltpu.VMEM((DN_HEADS, nc, HEAD_DIM, HEAD_DIM), F32),
            pltpu.VMEM((DN_HEADS, nc, HEAD_DIM, HEAD_DIM), BF16),
        ],
        compiler_params=pltpu.CompilerParams(
            dimension_semantics=("arbitrary", "arbitrary"),
            vmem_limit_bytes=MIXER_VMEM_BYTES),
        name=f"mixer_l{layer}",
    )(x, n1g, win, cw, gp, dng, scw, scg, wout)


def _ffn_call(x2d, n2g, wg, wu, wd, fg, layer, final_norm):
    rows, d_model = x2d.shape
    tm = min(FFN_ROW_TILE, rows)
    assert rows % tm == 0
    d_ff = wg.shape[-1]
    lay = lambda *rest: (lambda i: (layer, *rest))
    return pl.pallas_call(
        functools.partial(_ffn_kernel, final_norm=final_norm),
        grid=(rows // tm,),
        in_specs=[
            pl.BlockSpec((tm, d_model), lambda i: (i, 0)),
            _resident((None, 1, d_model), lay(0, 0)),
            _resident((None, d_model, d_ff), lay(0, 0)),
            _resident((None, d_model, d_ff), lay(0, 0)),
            _resident((None, d_ff, d_model), lay(0, 0)),
            _resident((1, d_model), lambda i: (0, 0)),
        ],
        out_specs=pl.BlockSpec((tm, d_model), lambda i: (i, 0)),
        out_shape=jax.ShapeDtypeStruct(x2d.shape, F32),
        compiler_params=pltpu.CompilerParams(
            dimension_semantics=("arbitrary",),
            vmem_limit_bytes=FFN_VMEM_BYTES),
        name=f"ffn_l{layer}",
    )(x2d, n2g, wg, wu, wd, fg)


def kernel(x, norm1_g, w_in, dn_conv_w, dn_a_log, dn_dt_bias, dn_norm_g, sc_conv_w, sc_norm_g,
           w_out, norm2_g, ffn_w_gate, ffn_w_up, ffn_w_down, final_norm_g):
    depth, d_model, _ = w_in.shape
    bsz, seq, _ = x.shape
    sc_width = sc_norm_g.shape[-1]

    o1 = QKV_COLS + DN_WIDTH
    o2 = o1 + 2 * DN_HEADS
    small = jnp.pad(w_in[..., o1:o2], ((0, 0), (0, 0), (0, LANES - 2 * DN_HEADS)))
    win = jnp.concatenate([w_in[..., :o1], w_in[..., o2:], small], axis=-1).astype(BF16)
    wout = w_out.astype(BF16)
    wg = ffn_w_gate.astype(BF16)
    wu = ffn_w_up.astype(BF16)
    wd = ffn_w_down.astype(BF16)

    gp = jnp.zeros((depth, SUBLANES, LANES), F32)
    gp = gp.at[:, 0, DECAY_LANE:DECAY_LANE + DN_HEADS].set(dn_a_log.astype(F32))
    gp = gp.at[:, 1, DECAY_LANE:DECAY_LANE + DN_HEADS].set(dn_dt_bias.astype(F32))

    n1g = norm1_g.reshape(depth, 1, d_model)
    n2g = norm2_g.reshape(depth, 1, d_model)
    dng = dn_norm_g.reshape(depth, 1, HEAD_DIM)
    scg = sc_norm_g.reshape(depth, 1, sc_width)
    fg = final_norm_g.reshape(1, d_model)

    for layer in range(depth):
        x = _mixer_call(x, n1g, win, dn_conv_w, gp, dng, sc_conv_w, scg, wout, layer)
        x = _ffn_call(x.reshape(bsz * seq, d_model), n2g, wg, wu, wd, fg, layer,
                      final_norm=(layer == depth - 1)).reshape(bsz, seq, d_model)
    return x
```
